```python
import math
import jax, jax.numpy as jnp
from jax import lax
import numpy as np

D_MODEL = 1024
BATCH = 32
SEQ = 2048
DEPTH = 2
DEC_BATCH = 2
DEC_SEQ = 16384
PAST_LEN = 128

GRID_W = 64
SSM_INNER = 1024
SSM_HEADS = 16
SSM_HEAD_DIM = 64
SSM_GROUPS = 4
SSM_HPG = SSM_HEADS // SSM_GROUPS
SSM_STATE = 64
CONV_WIDTH = 5
CONV_DIM = SSM_INNER + 2 * SSM_GROUPS * SSM_STATE
CHUNK = 128
NA_HEADS = 8
NA_HEAD_DIM = 64
NA_INNER = NA_HEADS * NA_HEAD_DIM
NA_ROWS = 8
NA_COLS = 16
DA_HEADS = 4
DA_HEAD_DIM = 64
DA_INNER = DA_HEADS * 2 * DA_HEAD_DIM
Q_BLOCK = 128
NORM_EPS = 1e-6
IN_SIZES = (SSM_INNER, CONV_DIM, 2 * SSM_HEADS, 3 * NA_INNER, NA_INNER, 2 * DA_INNER, DA_INNER, DA_INNER, 3 * D_MODEL)
D_IN = sum(IN_SIZES)

kernel_name = "hybrid_ssd_natten_diffattn_encoder"


def rms_norm(x, g):
    xf = x.astype(jnp.float32)
    y = xf * lax.rsqrt(jnp.mean(xf * xf, axis=-1, keepdims=True) + NORM_EPS)
    return (y * g.astype(jnp.float32)).astype(x.dtype)


def centred_conv(x, w):
    pad = CONV_WIDTH // 2
    return lax.conv_general_dilated(x, w[:, None, :].astype(x.dtype), window_strides=(1,), padding=[(pad, pad)],
                                    dimension_numbers=("NWC", "WIO", "NWC"), feature_group_count=x.shape[-1])


def ssd_scan(x, dt, a, bm, cm):
    b, L = x.shape[:2]
    nc = L // CHUNK
    f32 = jnp.float32
    xd = (x.astype(f32) * dt[..., None]).reshape(b, nc, CHUNK, SSM_GROUPS, SSM_HPG, SSM_HEAD_DIM)
    acum = jnp.cumsum((dt * a).reshape(b, nc, CHUNK, SSM_GROUPS, SSM_HPG), axis=2)
    bc = bm.astype(f32).reshape(b, nc, CHUNK, SSM_GROUPS, SSM_STATE)
    cc = cm.astype(f32).reshape(b, nc, CHUNK, SSM_GROUPS, SSM_STATE)
    lower = np.tril(np.ones((CHUNK, CHUNK), dtype=bool))[:, :, None, None]
    seg = acum[:, :, :, None] - acum[:, :, None, :]
    decay = jnp.where(lower, jnp.exp(jnp.where(lower, seg, 0.0)), 0.0)
    cb = jnp.einsum("bclgn,bcsgn->bclsg", cc, bc)
    y_diag = jnp.einsum("bclsge,bcsgep->bclgep", cb[..., None] * decay, xd)
    to_end = jnp.exp(acum[:, :, -1:] - acum)
    states = jnp.einsum("bclgn,bclge,bclgep->bcgepn", bc, to_end, xd)
    chunk_decay = jnp.exp(acum[:, :, -1])

    def step(h, inp):
        dec, st = inp
        return h * dec[..., None, None] + st, h

    h0 = jnp.zeros((b, SSM_GROUPS, SSM_HPG, SSM_HEAD_DIM, SSM_STATE), f32)
    _, h_prev = lax.scan(step, h0, (jnp.moveaxis(chunk_decay, 1, 0), jnp.moveaxis(states, 1, 0)))
    h_prev = jnp.moveaxis(h_prev, 0, 1)
    y_off = jnp.einsum("bclgn,bcgepn,bclge->bclgep", cc, h_prev, jnp.exp(acum))
    return (y_diag + y_off).reshape(b, L, SSM_GROUPS, SSM_HPG, SSM_HEAD_DIM).astype(x.dtype)


def mamba_branch(z, xbc, dt_raw, conv_w, conv_b, a_log, dt_bias, d_skip, norm_g):
    b, L, _ = xbc.shape
    xbc = jax.nn.silu(centred_conv(xbc, conv_w) + conv_b)
    xs, bm, cm = jnp.split(xbc, [SSM_INNER, SSM_INNER + SSM_GROUPS * SSM_STATE], axis=-1)
    xs = xs.reshape(b, L, SSM_GROUPS, SSM_HPG, SSM_HEAD_DIM)
    bm = bm.reshape(b, L, SSM_GROUPS, SSM_STATE)
    cm = cm.reshape(b, L, SSM_GROUPS, SSM_STATE)
    dt = jax.nn.softplus(dt_raw.astype(jnp.float32).reshape(b, L, 2, SSM_GROUPS, SSM_HPG)
                         + dt_bias.astype(jnp.float32).reshape(2, SSM_GROUPS, SSM_HPG))
    a = -jnp.exp(a_log.astype(jnp.float32)).reshape(2, SSM_GROUPS, SSM_HPG)
    flip = lambda t: jnp.flip(t, axis=1)
    y_fwd = ssd_scan(xs, dt[:, :, 0], a[0], bm, cm)
    y_bwd = flip(ssd_scan(flip(xs), flip(dt[:, :, 1]), a[1], flip(bm), flip(cm)))
    y = y_fwd + y_bwd + xs * d_skip.reshape(SSM_GROUPS, SSM_HPG, 1)
    y = y.reshape(b, L, SSM_INNER)
    return rms_norm(y * jax.nn.silu(z), norm_g)


def neighborhood_attention(q, k, v, rpb):
    b, L = q.shape[:2]
    rows = L // GRID_W
    kr = min(NA_ROWS, rows)
    ncb = GRID_W // NA_COLS
    qcol = np.arange(GRID_W).reshape(ncb, NA_COLS)
    band = np.clip(np.arange(ncb) * NA_COLS - NA_COLS // 2, 0, GRID_W - 2 * NA_COLS)
    kcol = band[:, None] + np.arange(2 * NA_COLS)
    win = np.clip(qcol - NA_COLS // 2, 0, GRID_W - NA_COLS)
    col_ok = (kcol[:, None, :] >= win[..., None]) & (kcol[:, None, :] < win[..., None] + NA_COLS)
    dc_idx = np.clip(kcol[:, None, :] - qcol[..., None], 1 - NA_COLS, NA_COLS - 1) + NA_COLS - 1
    q5 = q.reshape(b, rows, ncb, NA_COLS, NA_HEADS, NA_HEAD_DIM)
    k5 = k.reshape(b, rows, GRID_W, NA_HEADS, NA_HEAD_DIM)
    v5 = v.reshape(b, rows, GRID_W, NA_HEADS, NA_HEAD_DIM)
    scale = NA_HEAD_DIM ** -0.5

    def row_block(r):
        rs = jnp.clip(r - kr // 2, 0, rows - kr)
        qr = lax.dynamic_index_in_dim(q5, r, axis=1, keepdims=False)
        kb = lax.dynamic_slice_in_dim(k5, rs, kr, axis=1)[:, :, kcol]
        vb = lax.dynamic_slice_in_dim(v5, rs, kr, axis=1)[:, :, kcol]
        s = jnp.einsum("bnqhd,bjnkhd->bhnqjk", qr, kb, preferred_element_type=jnp.float32) * scale
        dr_idx = rs + jnp.arange(kr) - r + NA_ROWS - 1
        bias = jnp.take(rpb, dr_idx, axis=1)[:, :, dc_idx]
        s = s + jnp.transpose(bias, (0, 2, 3, 1, 4)).astype(jnp.float32)
        s = jnp.where(col_ok[None, None, :, :, None, :], s, -jnp.inf)
        p = jax.nn.softmax(s.reshape(s.shape[:4] + (-1,)), axis=-1).reshape(s.shape)
        return jnp.einsum("bhnqjk,bjnkhd->bnqhd", p.astype(v.dtype), vb)

    out = lax.map(row_block, jnp.arange(rows))
    return jnp.moveaxis(out, 0, 1).reshape(b, L, NA_HEADS, NA_HEAD_DIM)


def diff_attention(q, k, v, lam, slopes):
    b, L = q.shape[:2]
    pos = jnp.arange(L)
    scale = DA_HEAD_DIM ** -0.5

    def block(i):
        start = i * Q_BLOCK
        qs = lax.dynamic_slice_in_dim(q, start, Q_BLOCK, axis=1)
        s = jnp.einsum("bqhmd,bkhmd->bhmqk", qs, k, preferred_element_type=jnp.float32) * scale
        dist = jnp.abs(start + jnp.arange(Q_BLOCK)[:, None] - pos[None, :]).astype(jnp.float32)
        s = s - slopes[:, None, None, None] * dist
        p = jax.nn.softmax(s, axis=-1)
        a = p[:, :, 0] - lam * p[:, :, 1]
        return jnp.einsum("bhqk,bkhe->bqhe", a.astype(v.dtype), v)

    out = lax.map(block, jnp.arange(L // Q_BLOCK))
    return jnp.moveaxis(out, 0, 1).reshape(b, L, DA_HEADS, 2 * DA_HEAD_DIM)


def layer(x, l, norm_g, w_in, conv_w, conv_b, a_log, dt_bias, d_skip, ssm_norm_g, na_rpb, da_lambda, da_norm_g,
          w_proj_a, w_proj_b, w_proj_c, w_out):
    b, L, _ = x.shape
    h = rms_norm(x, norm_g[l])
    proj = h @ w_in[l]
    offsets = np.cumsum(IN_SIZES)[:-1].tolist()
    z, xbc, dt_raw, qkv_b, g_b, qk_c, v_c, g_c, gates = jnp.split(proj, offsets, axis=-1)
    y_a = mamba_branch(z, xbc, dt_raw, conv_w[l], conv_b[l], a_log[l], dt_bias[l], d_skip[l], ssm_norm_g[l])
    q_b, k_b, v_b = [t.reshape(b, L, NA_HEADS, NA_HEAD_DIM) for t in jnp.split(qkv_b, 3, axis=-1)]
    y_b = neighborhood_attention(q_b, k_b, v_b, na_rpb[l]).reshape(b, L, NA_INNER) * jax.nn.silu(g_b)
    q_c, k_c = [t.reshape(b, L, DA_HEADS, 2, DA_HEAD_DIM) for t in jnp.split(qk_c, 2, axis=-1)]
    v_c = v_c.reshape(b, L, DA_HEADS, 2 * DA_HEAD_DIM)
    lam_init = 0.8 - 0.6 * math.exp(-0.3 * l)
    lv = da_lambda[l].astype(jnp.float32)
    lam = jnp.exp(jnp.sum(lv[0] * lv[1])) - jnp.exp(jnp.sum(lv[2] * lv[3])) + lam_init
    slopes = 2.0 ** (-8.0 * jnp.arange(1, DA_HEADS + 1, dtype=jnp.float32) / DA_HEADS)
    o_c = rms_norm(diff_attention(q_c, k_c, v_c, lam, slopes), da_norm_g[l]) * (1.0 - lam_init)
    y_c = o_c.reshape(b, L, DA_INNER) * jax.nn.silu(g_c)
    gate_a, gate_b, gate_c = jnp.split(gates, 3, axis=-1)
    merged = (jax.nn.sigmoid(gate_a) * (y_a @ w_proj_a[l]) + jax.nn.sigmoid(gate_b) * (y_b @ w_proj_b[l])
              + jax.nn.sigmoid(gate_c) * (y_c @ w_proj_c[l]))
    return x + merged @ w_out[l]


def encoder(x, norm_g, w_in, conv_w, conv_b, a_log, dt_bias, d_skip, ssm_norm_g, na_rpb, da_lambda, da_norm_g,
            w_proj_a, w_proj_b, w_proj_c, w_out, final_norm_g):
    for l in range(DEPTH):
        x = layer(x, l, norm_g, w_in, conv_w, conv_b, a_log, dt_bias, d_skip, ssm_norm_g, na_rpb, da_lambda,
                  da_norm_g, w_proj_a, w_proj_b, w_proj_c, w_out)
    return rms_norm(x, final_norm_g)


def setup_inputs(seed: int = 0) -> dict:
    key = jax.random.key(seed)
    ks = jax.random.split(key, 20)
    f32 = jnp.float32
    nrm = lambda k, shape, s: jax.random.normal(k, shape, f32) * s
    dt0 = jnp.exp(jax.random.uniform(ks[6], (DEPTH, 2, SSM_HEADS), f32, math.log(1e-3), math.log(1e-1)))
    return {
        "x_prompt": nrm(ks[0], (BATCH, SEQ, D_MODEL), 1.0),
        "x_sample": nrm(ks[1], (DEC_BATCH, DEC_SEQ, D_MODEL), 1.0),
        "norm_g": 1.0 + nrm(ks[2], (DEPTH, D_MODEL), 0.02),
        "w_in": nrm(ks[3], (DEPTH, D_MODEL, D_IN), D_MODEL ** -0.5),
        "conv_w": nrm(ks[4], (DEPTH, CONV_WIDTH, CONV_DIM), CONV_WIDTH ** -0.5),
        "conv_b": nrm(ks[5], (DEPTH, CONV_DIM), 0.02),
        "a_log": jnp.log(jax.random.uniform(ks[7], (DEPTH, 2, SSM_HEADS), f32, 1.0, 16.0)),
        "dt_bias": dt0 + jnp.log(-jnp.expm1(-dt0)),
        "d_skip": 1.0 + nrm(ks[8], (DEPTH, SSM_HEADS), 0.02),
        "ssm_norm_g": 1.0 + nrm(ks[9], (DEPTH, SSM_INNER), 0.02),
        "na_rpb": nrm(ks[10], (DEPTH, NA_HEADS, 2 * NA_ROWS - 1, 2 * NA_COLS - 1), 0.1),
        "da_lambda": nrm(ks[11], (DEPTH, 4, DA_HEAD_DIM), 0.1),
        "da_norm_g": 1.0 + nrm(ks[12], (DEPTH, 2 * DA_HEAD_DIM), 0.02),
        "w_proj_a": nrm(ks[13], (DEPTH, SSM_INNER, D_MODEL), SSM_INNER ** -0.5),
        "w_proj_b": nrm(ks[14], (DEPTH, NA_INNER, D_MODEL), NA_INNER ** -0.5),
        "w_proj_c": nrm(ks[15], (DEPTH, DA_INNER, D_MODEL), DA_INNER ** -0.5),
        "w_out": nrm(ks[16], (DEPTH, D_MODEL, D_MODEL), D_MODEL ** -0.5),
        "final_norm_g": 1.0 + nrm(ks[17], (D_MODEL,), 0.02),
    }


def reference(x_prompt, x_sample, norm_g, w_in, conv_w, conv_b, a_log, dt_bias, d_skip, ssm_norm_g, na_rpb,
              da_lambda, da_norm_g, w_proj_a, w_proj_b, w_proj_c, w_out, final_norm_g):
    y_prompt = encoder(x_prompt, norm_g, w_in, conv_w, conv_b, a_log, dt_bias, d_skip, ssm_norm_g, na_rpb,
                       da_lambda, da_norm_g, w_proj_a, w_proj_b, w_proj_c, w_out, final_norm_g)
    y_sample = encoder(x_sample, norm_g, w_in, conv_w, conv_b, a_log, dt_bias, d_skip, ssm_norm_g, na_rpb,
                       da_lambda, da_norm_g, w_proj_a, w_proj_b, w_proj_c, w_out, final_norm_g)
    return (y_prompt, y_sample)
```

```python
import functools
import math

import numpy as np
import jax
import jax.numpy as jnp
from jax import lax
from jax.experimental import pallas as pl
from jax.experimental.pallas import tpu as pltpu

F32 = jnp.float32
BF16 = jnp.bfloat16

D_MODEL = 1024
DEPTH = 2
GRID_W = 64
SSM_INNER = 1024
SSM_HEADS = 16
SSM_HEAD_DIM = 64
SSM_GROUPS = 4
SSM_HPG = SSM_HEADS // SSM_GROUPS
SSM_STATE = 64
CONV_WIDTH = 5
BC_DIM = 2 * SSM_GROUPS * SSM_STATE
CONV_DIM = SSM_INNER + BC_DIM
CHUNK = 128
NA_HEADS = 8
NA_HEAD_DIM = 64
NA_INNER = NA_HEADS * NA_HEAD_DIM
NA_ROWS = 8
NA_COLS = 16
DA_HEADS = 4
DA_HEAD_DIM = 64
DA_INNER = DA_HEADS * 2 * DA_HEAD_DIM
NORM_EPS = 1e-6
IN_SIZES = (SSM_INNER, CONV_DIM, 2 * SSM_HEADS, 3 * NA_INNER, NA_INNER, 2 * DA_INNER, DA_INNER, DA_INNER,
            3 * D_MODEL)
D_IN = sum(IN_SIZES)

LANES = 128
SUBLANES = 8
NEG_BIG = -1e30
VMEM_LIMIT = 56 * 1024 * 1024

_o = np.cumsum((0,) + IN_SIZES)
_SEG_SRC = {
    "z": (_o[0], 1024), "xs": (_o[1], 1024), "bc": (_o[1] + 1024, 512), "dt": (_o[2], 32),
    "q_b": (_o[3], 512), "k_b": (_o[3] + 512, 512), "v_b": (_o[3] + 1024, 512), "g_b": (_o[4], 512),
    "q_c": (_o[5], 512), "k_c": (_o[5] + 512, 512), "v_c": (_o[6], 512), "g_c": (_o[7], 512),
    "ga": (_o[8], 1024), "gb": (_o[8] + 1024, 1024), "gc": (_o[8] + 2048, 1024),
}
_SEG_ORDER = ("z", "xs", "ga", "gb", "gc", "bc", "g_b", "q_b", "k_b", "v_b", "q_c", "k_c", "v_c", "g_c", "dt")
COL = {}
_perm = []
_off = 0
for _name in _SEG_ORDER:
    _src, _w = _SEG_SRC[_name]
    COL[_name] = _off
    _perm.extend(range(int(_src), int(_src) + _w))
    _off += _w
NP_USED = _off
NP = 10240
_PERM = np.asarray(_perm, dtype=np.int32)


def _silu(x):
    return x * (1.0 / (1.0 + jnp.exp(-x)))


def _sigmoid(x):
    return 1.0 / (1.0 + jnp.exp(-x))


def _softplus(x):
    return jnp.maximum(x, 0.0) + jnp.log(1.0 + jnp.exp(-jnp.abs(x)))


def _rms(x, g):
    return x * lax.rsqrt(jnp.mean(x * x, axis=-1, keepdims=True) + NORM_EPS) * g


def _dot(a, b):
    return jnp.dot(a, b, preferred_element_type=F32)


def _dot_nt(a, b):
    return lax.dot_general(a, b, (((1,), (1,)), ((), ())), preferred_element_type=F32)


def _inproj_kernel(x_ref, g_ref, w_ref, o_ref, h_ref):
    @pl.when(pl.program_id(1) == 0)
    def _():
        h_ref[...] = _rms(x_ref[...], g_ref[...]).astype(BF16)

    o_ref[...] = _dot(h_ref[...], w_ref[...])


def _inproj(x, g, w, tm=1024, tn=1024):
    t = x.shape[0]
    return pl.pallas_call(
        _inproj_kernel,
        grid=(t // tm, NP // tn),
        in_specs=[
            pl.BlockSpec((tm, D_MODEL), lambda i, j: (i, 0)),
            pl.BlockSpec((1, D_MODEL), lambda i, j: (0, 0)),
            pl.BlockSpec((D_MODEL, tn), lambda i, j: (0, j)),
        ],
        out_specs=pl.BlockSpec((tm, tn), lambda i, j: (i, j)),
        out_shape=jax.ShapeDtypeStruct((t, NP), F32),
        scratch_shapes=[pltpu.VMEM((tm, D_MODEL), BF16)],
        compiler_params=pltpu.CompilerParams(dimension_semantics=("parallel", "arbitrary"),
                                             vmem_limit_bytes=VMEM_LIMIT),
        name="inproj",
    )(x, g, w)


HALO = SUBLANES


def _ssd_kernel(reverse, *refs):
    if reverse:
        (xs_c, xs_p, xs_n, bc_c, bc_p, bc_n, dt_ref, cw_ref, cb_ref, dtb_ref, alog_ref, dsk_ref,
         yf_ref, z_ref, ng_ref, o_ref, xpad_ref, h_ref) = refs
    else:
        (xs_c, xs_p, xs_n, bc_c, bc_p, bc_n, dt_ref, cw_ref, cb_ref, dtb_ref, alog_ref, dsk_ref,
         o_ref, xpad_ref, h_ref) = refs
    c = pl.program_id(1)
    nc = pl.num_programs(1)
    cc = nc - 1 - c if reverse else c

    @pl.when(c == 0)
    def _():
        h_ref[...] = jnp.zeros_like(h_ref)

    prev_ok = (cc > 0).astype(F32)
    next_ok = (cc < nc - 1).astype(F32)
    xpad_ref[0:HALO, 0:SSM_INNER] = xs_p[...] * prev_ok
    xpad_ref[0:HALO, SSM_INNER:CONV_DIM] = bc_p[...] * prev_ok
    xpad_ref[HALO:HALO + CHUNK, 0:SSM_INNER] = xs_c[...]
    xpad_ref[HALO:HALO + CHUNK, SSM_INNER:CONV_DIM] = bc_c[...]
    xpad_ref[HALO + CHUNK:2 * HALO + CHUNK, 0:SSM_INNER] = xs_n[...] * next_ok
    xpad_ref[HALO + CHUNK:2 * HALO + CHUNK, SSM_INNER:CONV_DIM] = bc_n[...] * next_ok
    pad = CONV_WIDTH // 2
    acc = jnp.zeros((CHUNK, CONV_DIM), F32) + cb_ref[...]
    for k in range(CONV_WIDTH):
        lo = HALO - pad + k
        acc = acc + cw_ref[k:k + 1, :] * xpad_ref[lo:lo + CHUNK, :]
    xbc = _silu(acc)
    xs = xbc[:, 0:SSM_INNER]
    bm = xbc[:, SSM_INNER:SSM_INNER + BC_DIM // 2]
    cm = xbc[:, SSM_INNER + BC_DIM // 2:CONV_DIM]

    dtv = _softplus(dt_ref[...] + dtb_ref[...])
    dta = dtv * (-jnp.exp(alog_ref[...]))
    ri = lax.broadcasted_iota(jnp.int32, (CHUNK, CHUNK), 0)
    ci = lax.broadcasted_iota(jnp.int32, (CHUNK, CHUNK), 1)
    mask = (ci >= ri) if reverse else (ci <= ri)
    a_cum = jnp.dot(mask.astype(F32), dta, preferred_element_type=F32, precision=lax.Precision.HIGHEST)
    tot = a_cum[0:1, :] if reverse else a_cum[CHUNK - 1:CHUNK, :]
    a_cum_t = a_cum.T
    dt_t = dtv.T
    e_a = jnp.exp(a_cum)
    w_end = jnp.exp(tot - a_cum) * dtv
    e_tot = jnp.exp(tot)

    lane256 = lax.broadcasted_iota(jnp.int32, (1, 2 * LANES), 1)
    bm_t = bm.T
    bm16 = bm.astype(BF16)
    bm_t16 = bm_t.astype(BF16)
    hoff = SSM_HEADS if reverse else 0
    gw = SSM_HPG * SSM_HEAD_DIM
    for g in range(SSM_GROUPS):
        gmask = (lane256 >= g * SSM_STATE) & (lane256 < (g + 1) * SSM_STATE)
        cm_g16 = jnp.where(gmask, cm, 0.0).astype(BF16)
        cb = _dot_nt(cm_g16, bm16)
        xs_g = xs[:, g * gw:(g + 1) * gw]
        xs_g16 = xs_g.astype(BF16)
        y_g = jnp.zeros((CHUNK, gw), F32)
        w_cols = []
        e_cols = []
        d_cols = []
        for e in range(SSM_HPG):
            hl = hoff + g * SSM_HPG + e
            seg = a_cum[:, hl:hl + 1] - a_cum_t[hl:hl + 1, :]
            dec = jnp.exp(jnp.where(mask, seg, NEG_BIG))
            m_h = (cb * dec * dt_t[hl:hl + 1, :]).astype(BF16)
            hm = (lane256 >= e * SSM_HEAD_DIM) & (lane256 < (e + 1) * SSM_HEAD_DIM)
            y_g = y_g + jnp.where(hm, _dot(m_h, xs_g16), 0.0)
            w_cols.append((hm, w_end[:, hl:hl + 1]))
            e_cols.append((hm, e_a[:, hl:hl + 1]))
            d_cols.append((hm, e_tot[:, hl:hl + 1]))

        def expand(cols, rows):
            out = jnp.zeros((rows, gw), F32)
            for hm, col in cols:
                out = jnp.where(hm, col, out)
            return out

        w_g = expand(w_cols, CHUNK)
        e_g = expand(e_cols, CHUNK)
        d_g = expand(d_cols, 1)
        h_g = h_ref[g]
        h_pad = jnp.concatenate(
            [h_g if k == g else jnp.zeros_like(h_g) for k in range(SSM_GROUPS)], axis=0).astype(BF16)
        y_g = y_g + _dot(cm_g16, h_pad) * e_g
        s_full = _dot(bm_t16, (xs_g * w_g).astype(BF16))
        h_ref[g] = h_g * d_g + s_full[g * SSM_STATE:(g + 1) * SSM_STATE, :]
        if reverse:
            y_g = y_g + yf_ref[:, g * gw:(g + 1) * gw]
        else:
            y_g = y_g + xs_g * dsk_ref[:, g * gw:(g + 1) * gw]
        xpad_ref[HALO:HALO + CHUNK, g * gw:(g + 1) * gw] = y_g
    y = xpad_ref[HALO:HALO + CHUNK, 0:SSM_INNER]
    if reverse:
        o_ref[...] = _rms(y * _silu(z_ref[...]), ng_ref[...]).astype(o_ref.dtype)
    else:
        o_ref[...] = y


def _ssd(proj, batch, seqlen, reverse, conv_w, conv_b, dtb, alog, dsk, yf=None, ng=None):
    t = batch * seqlen
    nc = seqlen // CHUNK
    nh = t // HALO
    cpb = CHUNK // HALO

    def chunk(b, c):
        return b * nc + (nc - 1 - c if reverse else c)

    def cur(col, width):
        return pl.BlockSpec((CHUNK, width), lambda b, c: (chunk(b, c), col // width))

    def prev(col, width):
        return pl.BlockSpec((HALO, width), lambda b, c: (jnp.maximum(chunk(b, c) * cpb - 1, 0), col // width))

    def nxt(col, width):
        return pl.BlockSpec((HALO, width),
                            lambda b, c: (jnp.minimum((chunk(b, c) + 1) * cpb, nh - 1), col // width))

    def full(a):
        return pl.BlockSpec(a.shape, lambda b, c: (0,) * a.ndim)

    in_specs = [cur(COL["xs"], 1024), prev(COL["xs"], 1024), nxt(COL["xs"], 1024),
                cur(COL["bc"], 512), prev(COL["bc"], 512), nxt(COL["bc"], 512),
                cur(COL["dt"], LANES), full(conv_w), full(conv_b), full(dtb), full(alog), full(dsk)]
    args = [proj, proj, proj, proj, proj, proj, proj, conv_w, conv_b, dtb, alog, dsk]
    if reverse:
        in_specs += [pl.BlockSpec((CHUNK, SSM_INNER), lambda b, c: (chunk(b, c), 0)),
                     cur(COL["z"], 1024), full(ng)]
        args += [yf, proj, ng]
    return pl.pallas_call(
        functools.partial(_ssd_kernel, reverse),
        grid=(batch, nc),
        in_specs=in_specs,
        out_specs=pl.BlockSpec((CHUNK, SSM_INNER), lambda b, c: (chunk(b, c), 0)),
        out_shape=jax.ShapeDtypeStruct((t, SSM_INNER), BF16 if reverse else F32),
        scratch_shapes=[pltpu.VMEM((CHUNK + 2 * HALO, CONV_DIM), F32),
                        pltpu.VMEM((SSM_GROUPS, SSM_STATE, SSM_HPG * SSM_HEAD_DIM), F32)],
        compiler_params=pltpu.CompilerParams(dimension_semantics=("parallel", "arbitrary"),
                                             vmem_limit_bytes=VMEM_LIMIT),
        name="ssd_bwd" if reverse else "ssd_fwd",
    )(*args)


NA_QROWS = 8
NA_TOK = NA_QROWS * GRID_W
NA_WIN = NA_ROWS * GRID_W


def _na_kernel(q_ref, kp_ref, kc_ref, kn_ref, vp_ref, vc_ref, vn_ref, g_ref, bias_ref, o_ref, kbuf, vbuf):
    i = pl.program_id(1)
    rows = pl.num_programs(1) * NA_QROWS
    kbuf[0:NA_TOK, :] = kp_ref[...].astype(BF16)
    kbuf[NA_TOK:2 * NA_TOK, :] = kc_ref[...].astype(BF16)
    kbuf[2 * NA_TOK:3 * NA_TOK, :] = kn_ref[...].astype(BF16)
    vbuf[0:NA_TOK, :] = vp_ref[...].astype(BF16)
    vbuf[NA_TOK:2 * NA_TOK, :] = vc_ref[...].astype(BF16)
    vbuf[2 * NA_TOK:3 * NA_TOK, :] = vn_ref[...].astype(BF16)
    lane = lax.broadcasted_iota(jnp.int32, (1, LANES), 1)
    lo_half = lane < NA_HEAD_DIM
    scale = NA_HEAD_DIM ** -0.5

    def body(j, carry):
        r = i * NA_QROWS + j
        rs = jnp.clip(r - NA_ROWS // 2, 0, rows - NA_ROWS)
        off = pl.multiple_of((rs - i * NA_QROWS + NA_QROWS) * GRID_W, GRID_W)
        dr0 = rs - r + NA_ROWS - 1
        qoff = pl.multiple_of(j * GRID_W, GRID_W)
        for pr in range(NA_HEADS // 2):
            ls = slice(pr * LANES, (pr + 1) * LANES)
            q = q_ref[pl.ds(qoff, GRID_W), ls] * scale
            kw = kbuf[pl.ds(off, NA_WIN), ls]
            vw = vbuf[pl.ds(off, NA_WIN), ls]
            outs = []
            for hh in range(2):
                hm = lo_half if hh == 0 else jnp.logical_not(lo_half)
                qh = jnp.where(hm, q, 0.0).astype(BF16)
                s = _dot_nt(qh, kw) + bias_ref[dr0, 2 * pr + hh]
                m = jnp.max(s, axis=-1, keepdims=True)
                p = jnp.exp(s - m)
                l = jnp.sum(p, axis=-1, keepdims=True)
                outs.append(_dot(p.astype(BF16), vw) * (1.0 / l))
            o = jnp.where(lo_half, outs[0], outs[1])
            gate = _silu(g_ref[pl.ds(qoff, GRID_W), ls])
            o_ref[pl.ds(qoff, GRID_W), ls] = (o * gate).astype(o_ref.dtype)
        return carry

    lax.fori_loop(0, NA_QROWS, body, 0)


def _na_bias_table(rpb):
    qc = np.arange(GRID_W)
    kc = np.arange(GRID_W)
    win = np.clip(qc - NA_COLS // 2, 0, GRID_W - NA_COLS)
    ok = (kc[None, :] >= win[:, None]) & (kc[None, :] < win[:, None] + NA_COLS)
    dc = np.clip(kc[None, :] - qc[:, None], 1 - NA_COLS, NA_COLS - 1) + NA_COLS - 1
    t = rpb.astype(F32)[:, :, dc]
    t = jnp.where(ok[None, None], t, NEG_BIG)
    dr = np.arange(NA_ROWS)[:, None] + np.arange(NA_ROWS)[None, :]
    t = t[:, dr]
    t = jnp.transpose(t, (1, 0, 3, 2, 4))
    return t.reshape(NA_ROWS, NA_HEADS, GRID_W, NA_WIN)


def _na(proj, batch, seqlen, bias):
    t = batch * seqlen
    nb = seqlen // NA_TOK

    def blk(col, shift):
        def imap(b, i):
            return (b * nb + jnp.clip(i + shift, 0, nb - 1), col // NA_INNER)
        return pl.BlockSpec((NA_TOK, NA_INNER), imap)

    return pl.pallas_call(
        _na_kernel,
        grid=(batch, nb),
        in_specs=[blk(COL["q_b"], 0),
                  blk(COL["k_b"], -1), blk(COL["k_b"], 0), blk(COL["k_b"], 1),
                  blk(COL["v_b"], -1), blk(COL["v_b"], 0), blk(COL["v_b"], 1),
                  blk(COL["g_b"], 0),
                  pl.BlockSpec(bias.shape, lambda b, i: (0, 0, 0, 0))],
        out_specs=pl.BlockSpec((NA_TOK, NA_INNER), lambda b, i: (b * nb + i, 0)),
        out_shape=jax.ShapeDtypeStruct((t, NA_INNER), BF16),
        scratch_shapes=[pltpu.VMEM((3 * NA_TOK, NA_INNER), BF16), pltpu.VMEM((3 * NA_TOK, NA_INNER), BF16)],
        compiler_params=pltpu.CompilerParams(dimension_semantics=("parallel", "parallel"),
                                             vmem_limit_bytes=VMEM_LIMIT),
        name="na",
    )(proj, proj, proj, proj, proj, proj, proj, proj, bias)


def _da_kernel(tq, tk, out_scale, sc_ref, q_ref, k_ref, v_ref, g_ref, ng_ref, o_ref, m_ref, l_ref, acc_ref):
    h = pl.program_id(1)
    qi = pl.program_id(2)
    ki = pl.program_id(3)
    nk = pl.num_programs(3)

    @pl.when(ki == 0)
    def _():
        m_ref[...] = jnp.full_like(m_ref, NEG_BIG)
        l_ref[...] = jnp.zeros_like(l_ref)
        acc_ref[...] = jnp.zeros_like(acc_ref)

    slope = sc_ref[1 + h]
    lane = lax.broadcasted_iota(jnp.int32, (1, LANES), 1)
    q = q_ref[...] * (DA_HEAD_DIM ** -0.5)
    kb = k_ref[...].astype(BF16)
    vb = v_ref[...].astype(BF16)
    qpos = qi * tq + lax.broadcasted_iota(jnp.int32, (tq, tk), 0)
    kpos = ki * tk + lax.broadcasted_iota(jnp.int32, (tq, tk), 1)
    bias = jnp.abs(qpos - kpos).astype(F32) * slope
    for mi in range(2):
        hm = (lane < DA_HEAD_DIM) if mi == 0 else (lane >= DA_HEAD_DIM)
        qm = jnp.where(hm, q, 0.0).astype(BF16)
        s = _dot_nt(qm, kb) - bias
        m_prev = m_ref[mi]
        m_new = jnp.maximum(m_prev, jnp.max(s, axis=-1, keepdims=True))
        alpha = jnp.exp(m_prev - m_new)
        p = jnp.exp(s - m_new)
        l_ref[mi] = alpha * l_ref[mi] + jnp.sum(p, axis=-1, keepdims=True)
        acc_ref[mi] = alpha * acc_ref[mi] + _dot(p.astype(BF16), vb)
        m_ref[mi] = m_new

    @pl.when(ki == nk - 1)
    def _():
        lam = sc_ref[0]
        o = acc_ref[0] * (1.0 / l_ref[0]) - lam * (acc_ref[1] * (1.0 / l_ref[1]))
        o = _rms(o, ng_ref[...]) * out_scale
        o_ref[...] = (o * _silu(g_ref[...])).astype(o_ref.dtype)


def _da(proj, batch, seqlen, scal, ng, out_scale, tq=256, tk=512):
    t = batch * seqlen
    nq = seqlen // tq
    nk = seqlen // tk
    hw = 2 * DA_HEAD_DIM

    def qblk(col):
        return pl.BlockSpec((tq, hw), lambda b, h, i, j: (b * nq + i, col // hw + h))

    def kblk(col):
        return pl.BlockSpec((tk, hw), lambda b, h, i, j: (b * nk + j, col // hw + h))

    return pl.pallas_call(
        functools.partial(_da_kernel, tq, tk, out_scale),
        grid=(batch, DA_HEADS, nq, nk),
        in_specs=[pl.BlockSpec(memory_space=pltpu.SMEM),
                  qblk(COL["q_c"]), kblk(COL["k_c"]), kblk(COL["v_c"]), qblk(COL["g_c"]),
                  pl.BlockSpec((1, hw), lambda b, h, i, j: (0, 0))],
        out_specs=pl.BlockSpec((tq, hw), lambda b, h, i, j: (b * nq + i, h)),
        out_shape=jax.ShapeDtypeStruct((t, DA_INNER), BF16),
        scratch_shapes=[pltpu.VMEM((2, tq, 1), F32), pltpu.VMEM((2, tq, 1), F32), pltpu.VMEM((2, tq, hw), F32)],
        compiler_params=pltpu.CompilerParams(
            dimension_semantics=("parallel", "parallel", "parallel", "arbitrary"), vmem_limit_bytes=VMEM_LIMIT),
        name="da",
    )(scal, proj, proj, proj, proj, ng)


def _merge_kernel(final, ya_ref, yb_ref, yc_ref, ga_ref, gb_ref, gc_ref, x_ref, wa_ref, wb_ref, wc_ref, wo_ref,
                  fg_ref, o_ref):
    merged = (_sigmoid(ga_ref[...]) * _dot(ya_ref[...], wa_ref[...])
              + _sigmoid(gb_ref[...]) * _dot(yb_ref[...], wb_ref[...])
              + _sigmoid(gc_ref[...]) * _dot(yc_ref[...], wc_ref[...]))
    out = x_ref[...] + _dot(merged.astype(BF16), wo_ref[...])
    if final:
        out = _rms(out, fg_ref[...])
    o_ref[...] = out


def _merge(ya, yb, yc, proj, x, wa, wb, wc, wo, fg, final, tm=256):
    t = x.shape[0]

    def rows(width, col=0):
        return pl.BlockSpec((tm, width), lambda i: (i, col // width))

    def full(a):
        return pl.BlockSpec(a.shape, lambda i: (0,) * a.ndim)

    return pl.pallas_call(
        functools.partial(_merge_kernel, final),
        grid=(t // tm,),
        in_specs=[rows(SSM_INNER), rows(NA_INNER), rows(DA_INNER),
                  rows(D_MODEL, COL["ga"]), rows(D_MODEL, COL["gb"]), rows(D_MODEL, COL["gc"]),
                  rows(D_MODEL), full(wa), full(wb), full(wc), full(wo), full(fg)],
        out_specs=rows(D_MODEL),
        out_shape=jax.ShapeDtypeStruct((t, D_MODEL), F32),
        compiler_params=pltpu.CompilerParams(dimension_semantics=("parallel",), vmem_limit_bytes=VMEM_LIMIT),
        name="merge",
    )(ya, yb, yc, proj, proj, proj, x, wa, wb, wc, wo, fg)


def _pad_lanes(v, width=LANES):
    v = v.reshape(1, -1).astype(F32)
    return jnp.pad(v, ((0, 0), (0, width - v.shape[1])))


def _layer_params(l, norm_g, w_in, conv_w, conv_b, a_log, dt_bias, d_skip, ssm_norm_g, na_rpb, da_lambda,
                  da_norm_g, w_proj_a, w_proj_b, w_proj_c, w_out):
    w = jnp.pad(w_in[l][:, _PERM], ((0, 0), (0, NP - NP_USED))).astype(BF16)
    lam_init = 0.8 - 0.6 * math.exp(-0.3 * l)
    lv = da_lambda[l].astype(F32)
    lam = jnp.exp(jnp.sum(lv[0] * lv[1])) - jnp.exp(jnp.sum(lv[2] * lv[3])) + lam_init
    slopes = 2.0 ** (-8.0 * jnp.arange(1, DA_HEADS + 1, dtype=F32) / DA_HEADS)
    return dict(
        ng=norm_g[l].reshape(1, D_MODEL).astype(F32), w=w,
        conv_w=conv_w[l].astype(F32), conv_b=conv_b[l].reshape(1, CONV_DIM).astype(F32),
        dtb=_pad_lanes(dt_bias[l]), alog=_pad_lanes(a_log[l]),
        dsk=jnp.repeat(d_skip[l].astype(F32), SSM_HEAD_DIM).reshape(1, SSM_INNER),
        sng=ssm_norm_g[l].reshape(1, SSM_INNER).astype(F32),
        bias=_na_bias_table(na_rpb[l]),
        scal=jnp.concatenate([lam.reshape(1), slopes]).astype(F32),
        dng=da_norm_g[l].reshape(1, 2 * DA_HEAD_DIM).astype(F32), out_scale=1.0 - lam_init,
        wa=w_proj_a[l].astype(BF16), wb=w_proj_b[l].astype(BF16), wc=w_proj_c[l].astype(BF16),
        wo=w_out[l].astype(BF16),
    )


def _encoder(x, params, fg):
    batch, seqlen, _ = x.shape
    xt = x.reshape(batch * seqlen, D_MODEL)
    for l, p in enumerate(params):
        proj = _inproj(xt, p["ng"], p["w"])
        yf = _ssd(proj, batch, seqlen, False, p["conv_w"], p["conv_b"], p["dtb"], p["alog"], p["dsk"])
        ya = _ssd(proj, batch, seqlen, True, p["conv_w"], p["conv_b"], p["dtb"], p["alog"], p["dsk"],
                  yf=yf, ng=p["sng"])
        yb = _na(proj, batch, seqlen, p["bias"])
        yc = _da(proj, batch, seqlen, p["scal"], p["dng"], p["out_scale"])
        xt = _merge(ya, yb, yc, proj, xt, p["wa"], p["wb"], p["wc"], p["wo"], fg, l == len(params) - 1)
    return xt.reshape(batch, seqlen, D_MODEL)


def kernel(x_prompt, x_sample, norm_g, w_in, conv_w, conv_b, a_log, dt_bias, d_skip, ssm_norm_g, na_rpb,
           da_lambda, da_norm_g, w_proj_a, w_proj_b, w_proj_c, w_out, final_norm_g):
    params = [_layer_params(l, norm_g, w_in, conv_w, conv_b, a_log, dt_bias, d_skip, ssm_norm_g, na_rpb,
                            da_lambda, da_norm_g, w_proj_a, w_proj_b, w_proj_c, w_out) for l in range(DEPTH)]
    fg = final_norm_g.reshape(1, D_MODEL).astype(F32)
    return (_encoder(x_prompt, params, fg), _encoder(x_sample, params, fg))
```

```python
import functools
import math

import numpy as np
import jax
import jax.numpy as jnp
from jax import lax
from jax.experimental import pallas as pl
from jax.experimental.pallas import tpu as pltpu

F32 = jnp.float32
BF16 = jnp.bfloat16

D_MODEL = 1024
DEPTH = 2
GRID_W = 64
SSM_INNER = 1024
SSM_HEADS = 16
SSM_HEAD_DIM = 64
SSM_GROUPS = 4
SSM_HPG = SSM_HEADS // SSM_GROUPS
SSM_STATE = 64
CONV_WIDTH = 5
BC_DIM = 2 * SSM_GROUPS * SSM_STATE
CONV_DIM = SSM_INNER + BC_DIM
CHUNK = 128
NA_HEADS = 8
NA_HEAD_DIM = 64
NA_INNER = NA_HEADS * NA_HEAD_DIM
NA_ROWS = 8
NA_COLS = 16
DA_HEADS = 4
DA_HEAD_DIM = 64
DA_INNER = DA_HEADS * 2 * DA_HEAD_DIM
NORM_EPS = 1e-6
IN_SIZES = (SSM_INNER, CONV_DIM, 2 * SSM_HEADS, 3 * NA_INNER, NA_INNER, 2 * DA_INNER, DA_INNER, DA_INNER,
            3 * D_MODEL)
D_IN = sum(IN_SIZES)

LANES = 128
SUBLANES = 8
NEG_BIG = -1e30
VMEM_LIMIT = 56 * 1024 * 1024

_o = np.cumsum((0,) + IN_SIZES)
_SEG_SRC = {
    "z": (_o[0], 1024), "xs": (_o[1], 1024), "bc": (_o[1] + 1024, 512), "dt": (_o[2], 32),
    "q_b": (_o[3], 512), "k_b": (_o[3] + 512, 512), "v_b": (_o[3] + 1024, 512), "g_b": (_o[4], 512),
    "q_c": (_o[5], 512), "k_c": (_o[5] + 512, 512), "v_c": (_o[6], 512), "g_c": (_o[7], 512),
    "ga": (_o[8], 1024), "gb": (_o[8] + 1024, 1024), "gc": (_o[8] + 2048, 1024),
}
_SEG_ORDER = ("z", "xs", "ga", "gb", "gc", "bc", "g_b", "q_b", "k_b", "v_b", "q_c", "k_c", "v_c", "g_c", "dt")
COL = {}
_perm = []
_off = 0
for _name in _SEG_ORDER:
    _src, _w = _SEG_SRC[_name]
    COL[_name] = _off
    _perm.extend(range(int(_src), int(_src) + _w))
    _off += _w
NP_USED = _off
NP = 10240
_PERM = np.asarray(_perm, dtype=np.int32)


def _silu(x):
    return x * (1.0 / (1.0 + jnp.exp(-x)))


def _sigmoid(x):
    return 1.0 / (1.0 + jnp.exp(-x))


def _softplus(x):
    return jnp.maximum(x, 0.0) + jnp.log(1.0 + jnp.exp(-jnp.abs(x)))


def _rms(x, g):
    return x * lax.rsqrt(jnp.mean(x * x, axis=-1, keepdims=True) + NORM_EPS) * g


def _dot(a, b):
    return jnp.dot(a, b, preferred_element_type=F32)


def _dot_nt(a, b):
    return lax.dot_general(a, b, (((1,), (1,)), ((), ())), preferred_element_type=F32)


def _inproj_kernel(x_ref, g_ref, w_ref, o_ref, h_ref):
    @pl.when(pl.program_id(1) == 0)
    def _():
        h_ref[...] = _rms(x_ref[...], g_ref[...]).astype(BF16)

    o_ref[...] = _dot(h_ref[...], w_ref[...])


def _inproj(x, g, w, tm=1024, tn=1024):
    t = x.shape[0]
    return pl.pallas_call(
        _inproj_kernel,
        grid=(t // tm, NP // tn),
        in_specs=[
            pl.BlockSpec((tm, D_MODEL), lambda i, j: (i, 0)),
            pl.BlockSpec((1, D_MODEL), lambda i, j: (0, 0)),
            pl.BlockSpec((D_MODEL, tn), lambda i, j: (0, j)),
        ],
        out_specs=pl.BlockSpec((tm, tn), lambda i, j: (i, j)),
        out_shape=jax.ShapeDtypeStruct((t, NP), F32),
        scratch_shapes=[pltpu.VMEM((tm, D_MODEL), BF16)],
        compiler_params=pltpu.CompilerParams(dimension_semantics=("parallel", "arbitrary"),
                                             vmem_limit_bytes=VMEM_LIMIT),
        name="inproj",
    )(x, g, w)


HALO = SUBLANES


def _ssd_kernel(reverse, *refs):
    if reverse:
        (xs_c, xs_p, xs_n, bc_c, bc_p, bc_n, dt_ref, cw_ref, cb_ref, dtb_ref, alog_ref, dsk_ref,
         yf_ref, z_ref, ng_ref, o_ref, xpad_ref, h_ref) = refs
    else:
        (xs_c, xs_p, xs_n, bc_c, bc_p, bc_n, dt_ref, cw_ref, cb_ref, dtb_ref, alog_ref, dsk_ref,
         o_ref, xpad_ref, h_ref) = refs
    c = pl.program_id(1)
    nc = pl.num_programs(1)
    cc = nc - 1 - c if reverse else c

    @pl.when(c == 0)
    def _():
        h_ref[...] = jnp.zeros_like(h_ref)

    prev_ok = (cc > 0).astype(F32)
    next_ok = (cc < nc - 1).astype(F32)
    xpad_ref[0:HALO, 0:SSM_INNER] = xs_p[...] * prev_ok
    xpad_ref[0:HALO, SSM_INNER:CONV_DIM] = bc_p[...] * prev_ok
    xpad_ref[HALO:HALO + CHUNK, 0:SSM_INNER] = xs_c[...]
    xpad_ref[HALO:HALO + CHUNK, SSM_INNER:CONV_DIM] = bc_c[...]
    xpad_ref[HALO + CHUNK:2 * HALO + CHUNK, 0:SSM_INNER] = xs_n[...] * next_ok
    xpad_ref[HALO + CHUNK:2 * HALO + CHUNK, SSM_INNER:CONV_DIM] = bc_n[...] * next_ok
    pad = CONV_WIDTH // 2
    acc = jnp.zeros((CHUNK, CONV_DIM), F32) + cb_ref[...]
    for k in range(CONV_WIDTH):
        lo = HALO - pad + k
        acc = acc + cw_ref[k:k + 1, :] * xpad_ref[lo:lo + CHUNK, :]
    xbc = _silu(acc)
    xs = xbc[:, 0:SSM_INNER]
    bm = xbc[:, SSM_INNER:SSM_INNER + BC_DIM // 2]
    cm = xbc[:, SSM_INNER + BC_DIM // 2:CONV_DIM]

    dtv = _softplus(dt_ref[...] + dtb_ref[...])
    dta = dtv * (-jnp.exp(alog_ref[...]))
    ri = lax.broadcasted_iota(jnp.int32, (CHUNK, CHUNK), 0)
    ci = lax.broadcasted_iota(jnp.int32, (CHUNK, CHUNK), 1)
    mask = (ci >= ri) if reverse else (ci <= ri)
    a_cum = jnp.dot(mask.astype(F32), dta, preferred_element_type=F32, precision=lax.Precision.HIGHEST)
    tot = a_cum[0:1, :] if reverse else a_cum[CHUNK - 1:CHUNK, :]
    a_cum_t = a_cum.T
    dt_t = dtv.T
    e_a = jnp.exp(a_cum)
    w_end = jnp.exp(tot - a_cum) * dtv
    e_tot = jnp.exp(tot)

    lane256 = lax.broadcasted_iota(jnp.int32, (1, 2 * LANES), 1)
    bm_t = bm.T
    bm16 = bm.astype(BF16)
    bm_t16 = bm_t.astype(BF16)
    hoff = SSM_HEADS if reverse else 0
    gw = SSM_HPG * SSM_HEAD_DIM
    for g in range(SSM_GROUPS):
        gmask = (lane256 >= g * SSM_STATE) & (lane256 < (g + 1) * SSM_STATE)
        cm_g16 = jnp.where(gmask, cm, 0.0).astype(BF16)
        cb = _dot_nt(cm_g16, bm16)
        xs_g = xs[:, g * gw:(g + 1) * gw]
        xs_g16 = xs_g.astype(BF16)
        y_g = jnp.zeros((CHUNK, gw), F32)
        w_cols = []
        e_cols = []
        d_cols = []
        for e in range(SSM_HPG):
            hl = hoff + g * SSM_HPG + e
            seg = a_cum[:, hl:hl + 1] - a_cum_t[hl:hl + 1, :]
            dec = jnp.exp(jnp.where(mask, seg, NEG_BIG))
            m_h = (cb * dec * dt_t[hl:hl + 1, :]).astype(BF16)
            hm = (lane256 >= e * SSM_HEAD_DIM) & (lane256 < (e + 1) * SSM_HEAD_DIM)
            y_g = y_g + jnp.where(hm, _dot(m_h, xs_g16), 0.0)
            w_cols.append((hm, w_end[:, hl:hl + 1]))
            e_cols.append((hm, e_a[:, hl:hl + 1]))
            d_cols.append((hm, e_tot[:, hl:hl + 1]))

        def expand(cols, rows):
            out = jnp.zeros((rows, gw), F32)
            for hm, col in cols:
                out = jnp.where(hm, col, out)
            return out

        w_g = expand(w_cols, CHUNK)
        e_g = expand(e_cols, CHUNK)
        d_g = expand(d_cols, 1)
        h_g = h_ref[g]
        h_pad = jnp.concatenate(
            [h_g if k == g else jnp.zeros_like(h_g) for k in range(SSM_GROUPS)], axis=0).astype(BF16)
        y_g = y_g + _dot(cm_g16, h_pad) * e_g
        s_full = _dot(bm_t16, (xs_g * w_g).astype(BF16))
        h_ref[g] = h_g * d_g + s_full[g * SSM_STATE:(g + 1) * SSM_STATE, :]
        if reverse:
            y_g = y_g + yf_ref[:, g * gw:(g + 1) * gw]
        else:
            y_g = y_g + xs_g * dsk_ref[:, g * gw:(g + 1) * gw]
        xpad_ref[HALO:HALO + CHUNK, g * gw:(g + 1) * gw] = y_g
    y = xpad_ref[HALO:HALO + CHUNK, 0:SSM_INNER]
    if reverse:
        o_ref[...] = _rms(y * _silu(z_ref[...]), ng_ref[...]).astype(o_ref.dtype)
    else:
        o_ref[...] = y


def _ssd(proj, batch, seqlen, reverse, conv_w, conv_b, dtb, alog, dsk, yf=None, ng=None):
    t = batch * seqlen
    nc = seqlen // CHUNK
    nh = t // HALO
    cpb = CHUNK // HALO

    def chunk(b, c):
        return b * nc + (nc - 1 - c if reverse else c)

    def cur(col, width):
        return pl.BlockSpec((CHUNK, width), lambda b, c: (chunk(b, c), col // width))

    def prev(col, width):
        return pl.BlockSpec((HALO, width), lambda b, c: (jnp.maximum(chunk(b, c) * cpb - 1, 0), col // width))

    def nxt(col, width):
        return pl.BlockSpec((HALO, width),
                            lambda b, c: (jnp.minimum((chunk(b, c) + 1) * cpb, nh - 1), col // width))

    def full(a):
        return pl.BlockSpec(a.shape, lambda b, c: (0,) * a.ndim)

    in_specs = [cur(COL["xs"], 1024), prev(COL["xs"], 1024), nxt(COL["xs"], 1024),
                cur(COL["bc"], 512), prev(COL["bc"], 512), nxt(COL["bc"], 512),
                cur(COL["dt"], LANES), full(conv_w), full(conv_b), full(dtb), full(alog), full(dsk)]
    args = [proj, proj, proj, proj, proj, proj, proj, conv_w, conv_b, dtb, alog, dsk]
    if reverse:
        in_specs += [pl.BlockSpec((CHUNK, SSM_INNER), lambda b, c: (chunk(b, c), 0)),
                     cur(COL["z"], 1024), full(ng)]
        args += [yf, proj, ng]
    return pl.pallas_call(
        functools.partial(_ssd_kernel, reverse),
        grid=(batch, nc),
        in_specs=in_specs,
        out_specs=pl.BlockSpec((CHUNK, SSM_INNER), lambda b, c: (chunk(b, c), 0)),
        out_shape=jax.ShapeDtypeStruct((t, SSM_INNER), BF16 if reverse else F32),
        scratch_shapes=[pltpu.VMEM((CHUNK + 2 * HALO, CONV_DIM), F32),
                        pltpu.VMEM((SSM_GROUPS, SSM_STATE, SSM_HPG * SSM_HEAD_DIM), F32)],
        compiler_params=pltpu.CompilerParams(dimension_semantics=("parallel", "arbitrary"),
                                             vmem_limit_bytes=VMEM_LIMIT),
        name="ssd_bwd" if reverse else "ssd_fwd",
    )(*args)


NA_QROWS = 8
NA_TOK = NA_QROWS * GRID_W
NA_WIN = NA_ROWS * GRID_W


def _na_kernel(q_ref, kp_ref, kc_ref, kn_ref, vp_ref, vc_ref, vn_ref, g_ref, bias_ref, o_ref, kbuf, vbuf):
    i = pl.program_id(1)
    rows = pl.num_programs(1) * NA_QROWS
    kbuf[0:NA_TOK, :] = kp_ref[...].astype(BF16)
    kbuf[NA_TOK:2 * NA_TOK, :] = kc_ref[...].astype(BF16)
    kbuf[2 * NA_TOK:3 * NA_TOK, :] = kn_ref[...].astype(BF16)
    vbuf[0:NA_TOK, :] = vp_ref[...].astype(BF16)
    vbuf[NA_TOK:2 * NA_TOK, :] = vc_ref[...].astype(BF16)
    vbuf[2 * NA_TOK:3 * NA_TOK, :] = vn_ref[...].astype(BF16)
    lane = lax.broadcasted_iota(jnp.int32, (1, LANES), 1)
    lo_half = lane < NA_HEAD_DIM
    scale = NA_HEAD_DIM ** -0.5

    def body(j, carry):
        r = i * NA_QROWS + j
        rs = jnp.clip(r - NA_ROWS // 2, 0, rows - NA_ROWS)
        off = pl.multiple_of((rs - i * NA_QROWS + NA_QROWS) * GRID_W, GRID_W)
        dr0 = rs - r + NA_ROWS - 1
        qoff = pl.multiple_of(j * GRID_W, GRID_W)
        for pr in range(NA_HEADS // 2):
            ls = slice(pr * LANES, (pr + 1) * LANES)
            q = q_ref[pl.ds(qoff, GRID_W), ls] * scale
            kw = kbuf[pl.ds(off, NA_WIN), ls]
            vw = vbuf[pl.ds(off, NA_WIN), ls]
            outs = []
            for hh in range(2):
                hm = lo_half if hh == 0 else jnp.logical_not(lo_half)
                qh = jnp.where(hm, q, 0.0).astype(BF16)
                s = _dot_nt(qh, kw) + bias_ref[dr0, 2 * pr + hh]
                m = jnp.max(s, axis=-1, keepdims=True)
                p = jnp.exp(s - m)
                l = jnp.sum(p, axis=-1, keepdims=True)
                outs.append(_dot(p.astype(BF16), vw) * (1.0 / l))
            o = jnp.where(lo_half, outs[0], outs[1])
            gate = _silu(g_ref[pl.ds(qoff, GRID_W), ls])
            o_ref[pl.ds(qoff, GRID_W), ls] = (o * gate).astype(o_ref.dtype)
        return carry

    lax.fori_loop(0, NA_QROWS, body, 0)


def _na_bias_table(rpb):
    qc = np.arange(GRID_W)
    kc = np.arange(GRID_W)
    win = np.clip(qc - NA_COLS // 2, 0, GRID_W - NA_COLS)
    ok = (kc[None, :] >= win[:, None]) & (kc[None, :] < win[:, None] + NA_COLS)
    dc = np.clip(kc[None, :] - qc[:, None], 1 - NA_COLS, NA_COLS - 1) + NA_COLS - 1
    t = rpb.astype(F32)[:, :, dc]
    t = jnp.where(ok[None, None], t, NEG_BIG)
    dr = np.arange(NA_ROWS)[:, None] + np.arange(NA_ROWS)[None, :]
    t = t[:, dr]
    t = jnp.transpose(t, (1, 0, 3, 2, 4))
    return t.reshape(NA_ROWS, NA_HEADS, GRID_W, NA_WIN)


def _na(proj, batch, seqlen, bias):
    t = batch * seqlen
    nb = seqlen // NA_TOK

    def blk(col, shift):
        def imap(b, i):
            return (b * nb + jnp.clip(i + shift, 0, nb - 1), col // NA_INNER)
        return pl.BlockSpec((NA_TOK, NA_INNER), imap)

    return pl.pallas_call(
        _na_kernel,
        grid=(batch, nb),
        in_specs=[blk(COL["q_b"], 0),
                  blk(COL["k_b"], -1), blk(COL["k_b"], 0), blk(COL["k_b"], 1),
                  blk(COL["v_b"], -1), blk(COL["v_b"], 0), blk(COL["v_b"], 1),
                  blk(COL["g_b"], 0),
                  pl.BlockSpec(bias.shape, lambda b, i: (0, 0, 0, 0))],
        out_specs=pl.BlockSpec((NA_TOK, NA_INNER), lambda b, i: (b * nb + i, 0)),
        out_shape=jax.ShapeDtypeStruct((t, NA_INNER), BF16),
        scratch_shapes=[pltpu.VMEM((3 * NA_TOK, NA_INNER), BF16), pltpu.VMEM((3 * NA_TOK, NA_INNER), BF16)],
        compiler_params=pltpu.CompilerParams(dimension_semantics=("parallel", "parallel"),
                                             vmem_limit_bytes=VMEM_LIMIT),
        name="na",
    )(proj, proj, proj, proj, proj, proj, proj, proj, bias)


DA_BLK = 512
DA_ONES = 16
DA_HPS = 4
DA_AHEAD = 2
DA_SPLIT = 3
LOG2E = math.log2(math.e)


def _da_kernel(out_scale, sc_ref, q_ref, k_ref, v_ref, g_ref, ng_ref, kp_ref, o_ref, qt_ref, m_ref, acc_ref):
    blk = DA_BLK
    hw = 2 * DA_HEAD_DIM
    hp = pl.program_id(1)
    qi = pl.program_id(2)
    ki = pl.program_id(3)
    nk = pl.num_programs(3)
    heads = range(DA_HPS)

    def head_lanes(hh):
        return slice(hh * hw, (hh + 1) * hw)

    def attend(diag, side, rt_unit):
        kbs, vts, rts = [], [], []
        for hh in heads:
            slope = sc_ref[1 + hp * DA_HPS + hh]
            kb = k_ref[:, head_lanes(hh)].astype(BF16)
            vts.append(jnp.concatenate([v_ref[:, head_lanes(hh)].T.astype(BF16),
                                        jnp.ones((DA_ONES, blk), BF16)], axis=0))
            if diag:
                kbs.append(kb)
                rts.append(jnp.abs(lax.broadcasted_iota(jnp.int32, (blk, blk), 0)
                                   - lax.broadcasted_iota(jnp.int32, (blk, blk), 1)).astype(F32) * slope)
            else:
                kbs.append(jnp.concatenate([kb, kp_ref[side]], axis=1))
                rts.append(rt_unit * slope)

        def scores(hh, mi):
            if diag:
                return _dot(kbs[hh], qt_ref[hh, mi, 0:hw, :]) - rts[hh]
            return _dot(kbs[hh], qt_ref[hh, mi])

        chains = [(hh, mi) for hh in heads for mi in range(2)]
        pending = [scores(*c) for c in chains[:DA_AHEAD]]
        for n, (hh, mi) in enumerate(chains):
            t = pending.pop(0)
            if n + DA_AHEAD < len(chains):
                pending.append(scores(*chains[n + DA_AHEAD]))
            if diag:
                m_new = jnp.max(t, axis=0, keepdims=True)
                p = jnp.exp2(t - m_new).astype(BF16)
                acc_ref[hh, mi] = _dot(vts[hh], p)
            else:
                rt = rts[hh]
                m_prev = m_ref[hh, mi]
                m_new = jnp.maximum(m_prev, jnp.max(t, axis=0, keepdims=True) + rt)
                alpha = jnp.exp2(m_prev - m_new)
                p = jnp.exp2(t - (m_new - rt)).astype(BF16)
                acc_ref[hh, mi] = alpha * acc_ref[hh, mi] + _dot(vts[hh], p)
            m_ref[hh, mi] = m_new

    @pl.when(ki == 0)
    def _():
        lane = lax.broadcasted_iota(jnp.int32, (1, hw), 1)
        row = lax.broadcasted_iota(jnp.int32, (hw, blk), 0)
        for hh in heads:
            q = q_ref[:, head_lanes(hh)] * (DA_HEAD_DIM ** -0.5 * LOG2E)
            extra = jnp.zeros((hw, blk), F32)
            for part in range(DA_SPLIT):
                extra = jnp.where(row == part, sc_ref[1 + (1 + part) * DA_HEADS + hp * DA_HPS + hh], extra)
            extra = extra.astype(BF16)
            for mi in range(2):
                hm = (lane < DA_HEAD_DIM) if mi == 0 else (lane >= DA_HEAD_DIM)
                qt_ref[hh, mi, 0:hw, :] = jnp.where(hm, q, 0.0).T.astype(BF16)
                qt_ref[hh, mi, hw:2 * hw, :] = extra
        attend(True, None, None)

    @pl.when(ki > 0)
    def _():
        kblock = jnp.where(ki - 1 < qi, ki - 1, ki)
        before = kblock < qi
        sigma = jnp.where(before, 1.0, -1.0)
        side = jnp.where(before, 0, 1)
        rpos = lax.broadcasted_iota(jnp.int32, (1, blk), 1) + (qi - kblock) * blk - blk // 2
        attend(False, side, rpos.astype(F32) * (-sigma))

    @pl.when(ki == nk - 1)
    def _():
        lam = sc_ref[0]
        for hh in heads:
            o1 = acc_ref[hh, 0, 0:hw, :] * (1.0 / acc_ref[hh, 0, hw:hw + 1, :])
            o2 = acc_ref[hh, 1, 0:hw, :] * (1.0 / acc_ref[hh, 1, hw:hw + 1, :])
            o = (o1 - lam * o2).T
            o = _rms(o, ng_ref[...]) * out_scale
            o_ref[:, head_lanes(hh)] = (o * _silu(g_ref[:, head_lanes(hh)])).astype(o_ref.dtype)


def _da_key_positions():
    c = np.zeros((2, DA_BLK, 2 * DA_HEAD_DIM), np.float32)
    c[0, :, 0:DA_SPLIT] = (np.arange(DA_BLK) - DA_BLK // 2)[:, None]
    c[1] = -c[0]
    return jnp.asarray(c, dtype=BF16)


def _da(proj, batch, seqlen, scal, ng, out_scale):
    t = batch * seqlen
    blk = DA_BLK
    nb = seqlen // blk
    hw = 2 * DA_HEAD_DIM
    w = DA_HPS * hw
    kpos = _da_key_positions()

    def qblk(col):
        return pl.BlockSpec((blk, w), lambda b, h, i, j: (b * nb + i, col // w + h))

    def kblk(col):
        def imap(b, h, i, j):
            return (b * nb + jnp.where(j == 0, i, jnp.where(j - 1 < i, j - 1, j)), col // w + h)
        return pl.BlockSpec((blk, w), imap)

    return pl.pallas_call(
        functools.partial(_da_kernel, out_scale),
        grid=(batch, DA_HEADS // DA_HPS, nb, nb),
        in_specs=[pl.BlockSpec(memory_space=pltpu.SMEM),
                  qblk(COL["q_c"]), kblk(COL["k_c"]), kblk(COL["v_c"]), qblk(COL["g_c"]),
                  pl.BlockSpec((1, hw), lambda b, h, i, j: (0, 0)),
                  pl.BlockSpec(kpos.shape, lambda b, h, i, j: (0, 0, 0))],
        out_specs=pl.BlockSpec((blk, w), lambda b, h, i, j: (b * nb + i, h)),
        out_shape=jax.ShapeDtypeStruct((t, DA_INNER), BF16),
        scratch_shapes=[pltpu.VMEM((DA_HPS, 2, 2 * hw, blk), BF16),
                        pltpu.VMEM((DA_HPS, 2, 1, blk), F32),
                        pltpu.VMEM((DA_HPS, 2, hw + DA_ONES, blk), F32)],
        compiler_params=pltpu.CompilerParams(
            dimension_semantics=("parallel", "parallel", "parallel", "arbitrary"), vmem_limit_bytes=VMEM_LIMIT),
        name="da",
    )(scal, proj, proj, proj, proj, ng, kpos)


def _merge_kernel(final, ya_ref, yb_ref, yc_ref, ga_ref, gb_ref, gc_ref, x_ref, wa_ref, wb_ref, wc_ref, wo_ref,
                  fg_ref, o_ref):
    merged = (_sigmoid(ga_ref[...]) * _dot(ya_ref[...], wa_ref[...])
              + _sigmoid(gb_ref[...]) * _dot(yb_ref[...], wb_ref[...])
              + _sigmoid(gc_ref[...]) * _dot(yc_ref[...], wc_ref[...]))
    out = x_ref[...] + _dot(merged.astype(BF16), wo_ref[...])
    if final:
        out = _rms(out, fg_ref[...])
    o_ref[...] = out


def _merge(ya, yb, yc, proj, x, wa, wb, wc, wo, fg, final, tm=256):
    t = x.shape[0]

    def rows(width, col=0):
        return pl.BlockSpec((tm, width), lambda i: (i, col // width))

    def full(a):
        return pl.BlockSpec(a.shape, lambda i: (0,) * a.ndim)

    return pl.pallas_call(
        functools.partial(_merge_kernel, final),
        grid=(t // tm,),
        in_specs=[rows(SSM_INNER), rows(NA_INNER), rows(DA_INNER),
                  rows(D_MODEL, COL["ga"]), rows(D_MODEL, COL["gb"]), rows(D_MODEL, COL["gc"]),
                  rows(D_MODEL), full(wa), full(wb), full(wc), full(wo), full(fg)],
        out_specs=rows(D_MODEL),
        out_shape=jax.ShapeDtypeStruct((t, D_MODEL), F32),
        compiler_params=pltpu.CompilerParams(dimension_semantics=("parallel",), vmem_limit_bytes=VMEM_LIMIT),
        name="merge",
    )(ya, yb, yc, proj, proj, proj, x, wa, wb, wc, wo, fg)


def _pad_lanes(v, width=LANES):
    v = v.reshape(1, -1).astype(F32)
    return jnp.pad(v, ((0, 0), (0, width - v.shape[1])))


def _layer_params(l, norm_g, w_in, conv_w, conv_b, a_log, dt_bias, d_skip, ssm_norm_g, na_rpb, da_lambda,
                  da_norm_g, w_proj_a, w_proj_b, w_proj_c, w_out):
    w = jnp.pad(w_in[l][:, _PERM], ((0, 0), (0, NP - NP_USED))).astype(BF16)
    lam_init = 0.8 - 0.6 * math.exp(-0.3 * l)
    lv = da_lambda[l].astype(F32)
    lam = jnp.exp(jnp.sum(lv[0] * lv[1])) - jnp.exp(jnp.sum(lv[2] * lv[3])) + lam_init
    slopes = (2.0 ** (-8.0 * np.arange(1, DA_HEADS + 1, dtype=np.float64) / DA_HEADS) * LOG2E).astype(np.float32)
    parts, rest = [], jnp.asarray(slopes)
    for _ in range(DA_SPLIT):
        parts.append(rest.astype(BF16).astype(F32))
        rest = rest - parts[-1]
    slopes = jnp.concatenate([jnp.asarray(slopes)] + parts)
    return dict(
        ng=norm_g[l].reshape(1, D_MODEL).astype(F32), w=w,
        conv_w=conv_w[l].astype(F32), conv_b=conv_b[l].reshape(1, CONV_DIM).astype(F32),
        dtb=_pad_lanes(dt_bias[l]), alog=_pad_lanes(a_log[l]),
        dsk=jnp.repeat(d_skip[l].astype(F32), SSM_HEAD_DIM).reshape(1, SSM_INNER),
        sng=ssm_norm_g[l].reshape(1, SSM_INNER).astype(F32),
        bias=_na_bias_table(na_rpb[l]),
        scal=jnp.concatenate([lam.reshape(1), slopes]).astype(F32),
        dng=da_norm_g[l].reshape(1, 2 * DA_HEAD_DIM).astype(F32), out_scale=1.0 - lam_init,
        wa=w_proj_a[l].astype(BF16), wb=w_proj_b[l].astype(BF16), wc=w_proj_c[l].astype(BF16),
        wo=w_out[l].astype(BF16),
    )


def _encoder(x, params, fg):
    batch, seqlen, _ = x.shape
    xt = x.reshape(batch * seqlen, D_MODEL)
    for l, p in enumerate(params):
        proj = _inproj(xt, p["ng"], p["w"])
        yf = _ssd(proj, batch, seqlen, False, p["conv_w"], p["conv_b"], p["dtb"], p["alog"], p["dsk"])
        ya = _ssd(proj, batch, seqlen, True, p["conv_w"], p["conv_b"], p["dtb"], p["alog"], p["dsk"],
                  yf=yf, ng=p["sng"])
        yb = _na(proj, batch, seqlen, p["bias"])
        yc = _da(proj, batch, seqlen, p["scal"], p["dng"], p["out_scale"])
        xt = _merge(ya, yb, yc, proj, xt, p["wa"], p["wb"], p["wc"], p["wo"], fg, l == len(params) - 1)
    return xt.reshape(batch, seqlen, D_MODEL)


def kernel(x_prompt, x_sample, norm_g, w_in, conv_w, conv_b, a_log, dt_bias, d_skip, ssm_norm_g, na_rpb,
           da_lambda, da_norm_g, w_proj_a, w_proj_b, w_proj_c, w_out, final_norm_g):
    params = [_layer_params(l, norm_g, w_in, conv_w, conv_b, a_log, dt_bias, d_skip, ssm_norm_g, na_rpb,
                            da_lambda, da_norm_g, w_proj_a, w_proj_b, w_proj_c, w_out) for l in range(DEPTH)]
    fg = final_norm_g.reshape(1, D_MODEL).astype(F32)
    return (_encoder(x_prompt, params, fg), _encoder(x_sample, params, fg))
```

```python
import functools
import math

import numpy as np
import jax
import jax.numpy as jnp
from jax import lax
from jax.experimental import pallas as pl
from jax.experimental.pallas import tpu as pltpu

F32 = jnp.float32
BF16 = jnp.bfloat16

D_MODEL = 1024
DEPTH = 2
GRID_W = 64
SSM_INNER = 1024
SSM_HEADS = 16
SSM_HEAD_DIM = 64
SSM_GROUPS = 4
SSM_HPG = SSM_HEADS // SSM_GROUPS
SSM_STATE = 64
CONV_WIDTH = 5
BC_DIM = 2 * SSM_GROUPS * SSM_STATE
CONV_DIM = SSM_INNER + BC_DIM
CHUNK = 128
NA_HEADS = 8
NA_HEAD_DIM = 64
NA_INNER = NA_HEADS * NA_HEAD_DIM
NA_ROWS = 8
NA_COLS = 16
DA_HEADS = 4
DA_HEAD_DIM = 64
DA_INNER = DA_HEADS * 2 * DA_HEAD_DIM
NORM_EPS = 1e-6
IN_SIZES = (SSM_INNER, CONV_DIM, 2 * SSM_HEADS, 3 * NA_INNER, NA_INNER, 2 * DA_INNER, DA_INNER, DA_INNER,
            3 * D_MODEL)
D_IN = sum(IN_SIZES)

LANES = 128
SUBLANES = 8
NEG_BIG = -1e30
LOG2E = math.log2(math.e)
VMEM_LIMIT = 56 * 1024 * 1024

_o = np.cumsum((0,) + IN_SIZES)
_SEG_SRC = {
    "z": (_o[0], 1024), "xs": (_o[1], 1024), "bc": (_o[1] + 1024, 512), "dt": (_o[2], 32),
    "q_b": (_o[3], 512), "k_b": (_o[3] + 512, 512), "v_b": (_o[3] + 1024, 512), "g_b": (_o[4], 512),
    "q_c": (_o[5], 512), "k_c": (_o[5] + 512, 512), "v_c": (_o[6], 512), "g_c": (_o[7], 512),
    "ga": (_o[8], 1024), "gb": (_o[8] + 1024, 1024), "gc": (_o[8] + 2048, 1024),
}
PROJ_TN = 1024
_SEG_ORDER = {F32: ("z", "xs", "ga", "gb", "gc", "bc", "g_b", "g_c", "dt"),
              BF16: ("q_b", "k_b", "v_b", "q_c", "k_c", "v_c")}
_SEG_SCALE = {"q_b": NA_HEAD_DIM ** -0.5 * LOG2E, "q_c": DA_HEAD_DIM ** -0.5 * LOG2E}
COL = {}
_PERM = {}
_COLSCALE = {}
_WIDTH = {}
for _dt, _names in _SEG_ORDER.items():
    _perm, _scale, _off = [], [], 0
    for _name in _names:
        _src, _w = _SEG_SRC[_name]
        COL[_name] = _off
        _perm.extend(range(int(_src), int(_src) + _w))
        _scale.extend([_SEG_SCALE.get(_name, 1.0)] * _w)
        _off += _w
    _WIDTH[_dt] = -(-_off // PROJ_TN) * PROJ_TN
    _PERM[_dt] = np.asarray(_perm, dtype=np.int32)
    _COLSCALE[_dt] = np.asarray(_scale, dtype=np.float32)


def _silu(x):
    return x * (1.0 / (1.0 + jnp.exp(-x)))


def _sigmoid(x):
    return 1.0 / (1.0 + jnp.exp(-x))


def _softplus(x):
    return jnp.maximum(x, 0.0) + jnp.log(1.0 + jnp.exp(-jnp.abs(x)))


def _rms(x, g):
    return x * lax.rsqrt(jnp.mean(x * x, axis=-1, keepdims=True) + NORM_EPS) * g


def _dot(a, b):
    return jnp.dot(a, b, preferred_element_type=F32)


def _dot_nt(a, b):
    return lax.dot_general(a, b, (((1,), (1,)), ((), ())), preferred_element_type=F32)


def _inproj_kernel(x_ref, g_ref, w_ref, o_ref, h_ref):
    @pl.when(pl.program_id(1) == 0)
    def _():
        h_ref[...] = _rms(x_ref[...], g_ref[...]).astype(BF16)

    o_ref[...] = _dot(h_ref[...], w_ref[...]).astype(o_ref.dtype)


def _inproj(x, g, w, out_dtype, tm=2048):
    t = x.shape[0]
    tm = min(tm, t)
    tn = PROJ_TN
    return pl.pallas_call(
        _inproj_kernel,
        grid=(t // tm, w.shape[1] // tn),
        in_specs=[
            pl.BlockSpec((tm, D_MODEL), lambda i, j: (i, 0)),
            pl.BlockSpec((1, D_MODEL), lambda i, j: (0, 0)),
            pl.BlockSpec((D_MODEL, tn), lambda i, j: (0, j)),
        ],
        out_specs=pl.BlockSpec((tm, tn), lambda i, j: (i, j)),
        out_shape=jax.ShapeDtypeStruct((t, w.shape[1]), out_dtype),
        scratch_shapes=[pltpu.VMEM((tm, D_MODEL), BF16)],
        compiler_params=pltpu.CompilerParams(dimension_semantics=("parallel", "arbitrary"),
                                             vmem_limit_bytes=VMEM_LIMIT),
        name="inproj",
    )(x, g, w)


HALO = SUBLANES


def _ssd_kernel(reverse, *refs):
    if reverse:
        (xbc_ref, dt_ref, dtb_ref, alog_ref, yf_ref, z_ref, ng_ref, o_ref, xpad_ref, h_ref) = refs
    else:
        (xs_c, xs_p, xs_n, bc_c, bc_p, bc_n, dt_ref, cw_ref, cb_ref, dtb_ref, alog_ref, dsk_ref,
         o_ref, xbc_out_ref, xpad_ref, h_ref) = refs
    c = pl.program_id(1)
    nc = pl.num_programs(1)

    @pl.when(c == 0)
    def _():
        h_ref[...] = jnp.zeros_like(h_ref)

    if reverse:
        xbc = xbc_ref[...].astype(F32)
    else:
        prev_ok = (c > 0).astype(F32)
        next_ok = (c < nc - 1).astype(F32)
        xpad_ref[0:HALO, 0:SSM_INNER] = xs_p[...] * prev_ok
        xpad_ref[0:HALO, SSM_INNER:CONV_DIM] = bc_p[...] * prev_ok
        xpad_ref[HALO:HALO + CHUNK, 0:SSM_INNER] = xs_c[...]
        xpad_ref[HALO:HALO + CHUNK, SSM_INNER:CONV_DIM] = bc_c[...]
        xpad_ref[HALO + CHUNK:2 * HALO + CHUNK, 0:SSM_INNER] = xs_n[...] * next_ok
        xpad_ref[HALO + CHUNK:2 * HALO + CHUNK, SSM_INNER:CONV_DIM] = bc_n[...] * next_ok
        pad = CONV_WIDTH // 2
        acc = jnp.zeros((CHUNK, CONV_DIM), F32) + cb_ref[...]
        for k in range(CONV_WIDTH):
            lo = HALO - pad + k
            acc = acc + cw_ref[k:k + 1, :] * xpad_ref[lo:lo + CHUNK, :]
        xbc = _silu(acc)
        xbc_out_ref[...] = xbc.astype(xbc_out_ref.dtype)
    xs = xbc[:, 0:SSM_INNER]
    bm = xbc[:, SSM_INNER:SSM_INNER + BC_DIM // 2]
    cm = xbc[:, SSM_INNER + BC_DIM // 2:CONV_DIM]

    dtv = _softplus(dt_ref[...] + dtb_ref[...])
    dta = dtv * (-jnp.exp(alog_ref[...]))
    ri = lax.broadcasted_iota(jnp.int32, (CHUNK, CHUNK), 0)
    ci = lax.broadcasted_iota(jnp.int32, (CHUNK, CHUNK), 1)
    mask = (ci >= ri) if reverse else (ci <= ri)
    a_cum = jnp.dot(mask.astype(F32), dta, preferred_element_type=F32, precision=lax.Precision.HIGHEST)
    tot = a_cum[0:1, :] if reverse else a_cum[CHUNK - 1:CHUNK, :]
    a_cum_t = a_cum.T
    dt_t = dtv.T
    e_a = jnp.exp(a_cum)
    w_end = jnp.exp(tot - a_cum) * dtv
    e_tot = jnp.exp(tot)

    lane256 = lax.broadcasted_iota(jnp.int32, (1, 2 * LANES), 1)
    bm_t = bm.T
    bm16 = bm.astype(BF16)
    bm_t16 = bm_t.astype(BF16)
    hoff = SSM_HEADS if reverse else 0
    gw = SSM_HPG * SSM_HEAD_DIM
    for g in range(SSM_GROUPS):
        gmask = (lane256 >= g * SSM_STATE) & (lane256 < (g + 1) * SSM_STATE)
        cm_g16 = jnp.where(gmask, cm, 0.0).astype(BF16)
        cb = _dot_nt(cm_g16, bm16)
        xs_g = xs[:, g * gw:(g + 1) * gw]
        xs_g16 = xs_g.astype(BF16)
        y_g = jnp.zeros((CHUNK, gw), F32)
        w_cols = []
        e_cols = []
        d_cols = []
        for e in range(SSM_HPG):
            hl = hoff + g * SSM_HPG + e
            seg = a_cum[:, hl:hl + 1] - a_cum_t[hl:hl + 1, :]
            dec = jnp.exp(jnp.where(mask, seg, NEG_BIG))
            m_h = (cb * dec * dt_t[hl:hl + 1, :]).astype(BF16)
            hm = (lane256 >= e * SSM_HEAD_DIM) & (lane256 < (e + 1) * SSM_HEAD_DIM)
            y_g = y_g + jnp.where(hm, _dot(m_h, xs_g16), 0.0)
            w_cols.append((hm, w_end[:, hl:hl + 1]))
            e_cols.append((hm, e_a[:, hl:hl + 1]))
            d_cols.append((hm, e_tot[:, hl:hl + 1]))

        def expand(cols, rows):
            out = jnp.zeros((rows, gw), F32)
            for hm, col in cols:
                out = jnp.where(hm, col, out)
            return out

        w_g = expand(w_cols, CHUNK)
        e_g = expand(e_cols, CHUNK)
        d_g = expand(d_cols, 1)
        h_g = h_ref[g]
        h_pad = jnp.concatenate(
            [h_g if k == g else jnp.zeros_like(h_g) for k in range(SSM_GROUPS)], axis=0).astype(BF16)
        y_g = y_g + _dot(cm_g16, h_pad) * e_g
        s_full = _dot(bm_t16, (xs_g * w_g).astype(BF16))
        h_ref[g] = h_g * d_g + s_full[g * SSM_STATE:(g + 1) * SSM_STATE, :]
        if reverse:
            y_g = y_g + yf_ref[:, g * gw:(g + 1) * gw]
        else:
            y_g = y_g + xs_g * dsk_ref[:, g * gw:(g + 1) * gw]
        xpad_ref[HALO:HALO + CHUNK, g * gw:(g + 1) * gw] = y_g
    y = xpad_ref[HALO:HALO + CHUNK, 0:SSM_INNER]
    if reverse:
        o_ref[...] = _rms(y * _silu(z_ref[...]), ng_ref[...]).astype(o_ref.dtype)
    else:
        o_ref[...] = y


def _ssd(proj, batch, seqlen, reverse, conv_w, conv_b, dtb, alog, dsk, yf=None, xbc=None, ng=None):
    t = batch * seqlen
    nc = seqlen // CHUNK
    nh = t // HALO
    cpb = CHUNK // HALO

    def chunk(b, c):
        return b * nc + (nc - 1 - c if reverse else c)

    def cur(col, width):
        return pl.BlockSpec((CHUNK, width), lambda b, c: (chunk(b, c), col // width))

    def prev(col, width):
        return pl.BlockSpec((HALO, width), lambda b, c: (jnp.maximum(chunk(b, c) * cpb - 1, 0), col // width))

    def nxt(col, width):
        return pl.BlockSpec((HALO, width),
                            lambda b, c: (jnp.minimum((chunk(b, c) + 1) * cpb, nh - 1), col // width))

    def full(a):
        return pl.BlockSpec(a.shape, lambda b, c: (0,) * a.ndim)

    def rows(width):
        return pl.BlockSpec((CHUNK, width), lambda b, c: (chunk(b, c), 0))

    if reverse:
        in_specs = [rows(CONV_DIM), cur(COL["dt"], LANES), full(dtb), full(alog),
                    rows(SSM_INNER), cur(COL["z"], 1024), full(ng)]
        args = [xbc, proj, dtb, alog, yf, proj, ng]
        out_specs = rows(SSM_INNER)
        out_shape = jax.ShapeDtypeStruct((t, SSM_INNER), BF16)
    else:
        in_specs = [cur(COL["xs"], 1024), prev(COL["xs"], 1024), nxt(COL["xs"], 1024),
                    cur(COL["bc"], 512), prev(COL["bc"], 512), nxt(COL["bc"], 512),
                    cur(COL["dt"], LANES), full(conv_w), full(conv_b), full(dtb), full(alog), full(dsk)]
        args = [proj, proj, proj, proj, proj, proj, proj, conv_w, conv_b, dtb, alog, dsk]
        out_specs = [rows(SSM_INNER), rows(CONV_DIM)]
        out_shape = [jax.ShapeDtypeStruct((t, SSM_INNER), F32), jax.ShapeDtypeStruct((t, CONV_DIM), BF16)]
    return pl.pallas_call(
        functools.partial(_ssd_kernel, reverse),
        grid=(batch, nc),
        in_specs=in_specs,
        out_specs=out_specs,
        out_shape=out_shape,
        scratch_shapes=[pltpu.VMEM((CHUNK + 2 * HALO, CONV_DIM), F32),
                        pltpu.VMEM((SSM_GROUPS, SSM_STATE, SSM_HPG * SSM_HEAD_DIM), F32)],
        compiler_params=pltpu.CompilerParams(dimension_semantics=("parallel", "arbitrary"),
                                             vmem_limit_bytes=VMEM_LIMIT),
        name="ssd_bwd" if reverse else "ssd_fwd",
    )(*args)


NA_QROWS = 8
NA_TOK = NA_QROWS * GRID_W
NA_WIN = NA_ROWS * GRID_W
NA_AHEAD = 2


def _na_kernel(q_ref, kp_ref, kc_ref, kn_ref, vp_ref, vc_ref, vn_ref, g_ref, bias_ref, o_ref, kbuf, vbuf):
    i = pl.program_id(1)
    rows = pl.num_programs(1) * NA_QROWS
    kbuf[0:NA_TOK, :] = kp_ref[...]
    kbuf[NA_TOK:2 * NA_TOK, :] = kc_ref[...]
    kbuf[2 * NA_TOK:3 * NA_TOK, :] = kn_ref[...]
    vbuf[0:NA_TOK, :] = vp_ref[...]
    vbuf[NA_TOK:2 * NA_TOK, :] = vc_ref[...]
    vbuf[2 * NA_TOK:3 * NA_TOK, :] = vn_ref[...]
    lane = lax.broadcasted_iota(jnp.int32, (1, LANES), 1)
    lo_half = lane < NA_HEAD_DIM
    ones = jnp.ones((NA_WIN, LANES), BF16)

    def window(j):
        r = i * NA_QROWS + j
        rs = jnp.clip(r - NA_ROWS // 2, 0, rows - NA_ROWS)
        off = pl.multiple_of((rs - i * NA_QROWS + NA_QROWS) * GRID_W, GRID_W)
        return off, rs - r + NA_ROWS - 1

    def scores(j, pr):
        off, dr0 = window(j)
        ls = slice(pr * LANES, (pr + 1) * LANES)
        q = q_ref[j * GRID_W:(j + 1) * GRID_W, ls]
        zero = jnp.zeros_like(q)
        qs = jnp.concatenate([jnp.where(lo_half, q, zero), jnp.where(lo_half, zero, q)], axis=0)
        return _dot_nt(qs, kbuf[pl.ds(off, NA_WIN), ls]) + bias_ref[dr0, pr]

    def finish(j, pr, s):
        off, _ = window(j)
        ls = slice(pr * LANES, (pr + 1) * LANES)
        p = jnp.exp2(s - jnp.max(s, axis=-1, keepdims=True)).astype(BF16)
        o = _dot(p, jnp.concatenate([vbuf[pl.ds(off, NA_WIN), ls], ones], axis=1))
        o = o[:, 0:LANES] * (1.0 / o[:, LANES:2 * LANES])
        o = jnp.where(lo_half, o[0:GRID_W], o[GRID_W:2 * GRID_W])
        gate = _silu(g_ref[j * GRID_W:(j + 1) * GRID_W, ls])
        o_ref[j * GRID_W:(j + 1) * GRID_W, ls] = (o * gate).astype(o_ref.dtype)

    units = [(j, pr) for j in range(NA_QROWS) for pr in range(NA_HEADS // 2)]
    pending = [scores(*u) for u in units[:NA_AHEAD]]
    for n, u in enumerate(units):
        s = pending.pop(0)
        if n + NA_AHEAD < len(units):
            pending.append(scores(*units[n + NA_AHEAD]))
        finish(*u, s)


def _na_bias_table(rpb):
    qc = np.arange(GRID_W)
    kc = np.arange(GRID_W)
    win = np.clip(qc - NA_COLS // 2, 0, GRID_W - NA_COLS)
    ok = (kc[None, :] >= win[:, None]) & (kc[None, :] < win[:, None] + NA_COLS)
    dc = np.clip(kc[None, :] - qc[:, None], 1 - NA_COLS, NA_COLS - 1) + NA_COLS - 1
    t = rpb.astype(F32)[:, :, dc]
    t = jnp.where(ok[None, None], t, NEG_BIG)
    dr = np.arange(NA_ROWS)[:, None] + np.arange(NA_ROWS)[None, :]
    t = t[:, dr]
    t = jnp.transpose(t, (1, 0, 3, 2, 4))
    return t.reshape(NA_ROWS, NA_HEADS // 2, 2 * GRID_W, NA_WIN) * LOG2E


def _na(proj, qkv, batch, seqlen, bias):
    t = batch * seqlen
    nb = seqlen // NA_TOK

    def blk(col, shift):
        def imap(b, i):
            return (b * nb + jnp.clip(i + shift, 0, nb - 1), col // NA_INNER)
        return pl.BlockSpec((NA_TOK, NA_INNER), imap)

    return pl.pallas_call(
        _na_kernel,
        grid=(batch, nb),
        in_specs=[blk(COL["q_b"], 0),
                  blk(COL["k_b"], -1), blk(COL["k_b"], 0), blk(COL["k_b"], 1),
                  blk(COL["v_b"], -1), blk(COL["v_b"], 0), blk(COL["v_b"], 1),
                  blk(COL["g_b"], 0),
                  pl.BlockSpec(bias.shape, lambda b, i: (0, 0, 0, 0))],
        out_specs=pl.BlockSpec((NA_TOK, NA_INNER), lambda b, i: (b * nb + i, 0)),
        out_shape=jax.ShapeDtypeStruct((t, NA_INNER), BF16),
        scratch_shapes=[pltpu.VMEM((3 * NA_TOK, NA_INNER), BF16), pltpu.VMEM((3 * NA_TOK, NA_INNER), BF16)],
        compiler_params=pltpu.CompilerParams(dimension_semantics=("parallel", "parallel"),
                                             vmem_limit_bytes=VMEM_LIMIT),
        name="na",
    )(qkv, qkv, qkv, qkv, qkv, qkv, qkv, proj, bias)


DA_BLK = 512
DA_ONES = 16
DA_HPS = 4
DA_AHEAD = 2
DA_SPLIT = 3


def _da_kernel(out_scale, sc_ref, q_ref, k_ref, v_ref, g_ref, ng_ref, kp_ref, o_ref, qt_ref, m_ref, acc_ref):
    blk = DA_BLK
    hw = 2 * DA_HEAD_DIM
    hp = pl.program_id(1)
    qi = pl.program_id(2)
    ki = pl.program_id(3)
    nk = pl.num_programs(3)
    heads = range(DA_HPS)

    def head_lanes(hh):
        return slice(hh * hw, (hh + 1) * hw)

    def attend(diag, side, rt_unit):
        kbs, vts, rts = [], [], []
        for hh in heads:
            slope = sc_ref[1 + hp * DA_HPS + hh]
            kb = k_ref[:, head_lanes(hh)]
            vts.append(jnp.concatenate([v_ref[:, head_lanes(hh)].astype(F32).T.astype(BF16),
                                        jnp.ones((DA_ONES, blk), BF16)], axis=0))
            if diag:
                kbs.append(kb)
                rts.append(jnp.abs(lax.broadcasted_iota(jnp.int32, (blk, blk), 0)
                                   - lax.broadcasted_iota(jnp.int32, (blk, blk), 1)).astype(F32) * slope)
            else:
                kbs.append(jnp.concatenate([kb, kp_ref[side]], axis=1))
                rts.append(rt_unit * slope)

        def scores(hh, mi):
            if diag:
                return _dot(kbs[hh], qt_ref[hh, mi, 0:hw, :]) - rts[hh]
            return _dot(kbs[hh], qt_ref[hh, mi])

        chains = [(hh, mi) for hh in heads for mi in range(2)]
        pending = [scores(*c) for c in chains[:DA_AHEAD]]
        for n, (hh, mi) in enumerate(chains):
            t = pending.pop(0)
            if n + DA_AHEAD < len(chains):
                pending.append(scores(*chains[n + DA_AHEAD]))
            if diag:
                m_new = jnp.max(t, axis=0, keepdims=True)
                p = jnp.exp2(t - m_new).astype(BF16)
                acc_ref[hh, mi] = _dot(vts[hh], p)
            else:
                rt = rts[hh]
                m_prev = m_ref[hh, mi]
                m_new = jnp.maximum(m_prev, jnp.max(t, axis=0, keepdims=True) + rt)
                alpha = jnp.exp2(m_prev - m_new)
                p = jnp.exp2(t - (m_new - rt)).astype(BF16)
                acc_ref[hh, mi] = alpha * acc_ref[hh, mi] + _dot(vts[hh], p)
            m_ref[hh, mi] = m_new

    @pl.when(ki == 0)
    def _():
        lane = lax.broadcasted_iota(jnp.int32, (1, hw), 1)
        row = lax.broadcasted_iota(jnp.int32, (hw, blk), 0)
        for hh in heads:
            q = q_ref[:, head_lanes(hh)].astype(F32)
            extra = jnp.zeros((hw, blk), F32)
            for part in range(DA_SPLIT):
                extra = jnp.where(row == part, sc_ref[1 + (1 + part) * DA_HEADS + hp * DA_HPS + hh], extra)
            extra = extra.astype(BF16)
            for mi in range(2):
                hm = (lane < DA_HEAD_DIM) if mi == 0 else (lane >= DA_HEAD_DIM)
                qt_ref[hh, mi, 0:hw, :] = jnp.where(hm, q, 0.0).T.astype(BF16)
                qt_ref[hh, mi, hw:2 * hw, :] = extra
        attend(True, None, None)

    @pl.when(ki > 0)
    def _():
        kblock = jnp.where(ki - 1 < qi, ki - 1, ki)
        before = kblock < qi
        sigma = jnp.where(before, 1.0, -1.0)
        side = jnp.where(before, 0, 1)
        rpos = lax.broadcasted_iota(jnp.int32, (1, blk), 1) + (qi - kblock) * blk - blk // 2
        attend(False, side, rpos.astype(F32) * (-sigma))

    @pl.when(ki == nk - 1)
    def _():
        lam = sc_ref[0]
        for hh in heads:
            o1 = acc_ref[hh, 0, 0:hw, :] * (1.0 / acc_ref[hh, 0, hw:hw + 1, :])
            o2 = acc_ref[hh, 1, 0:hw, :] * (1.0 / acc_ref[hh, 1, hw:hw + 1, :])
            o = (o1 - lam * o2).T
            o = _rms(o, ng_ref[...]) * out_scale
            o_ref[:, head_lanes(hh)] = (o * _silu(g_ref[:, head_lanes(hh)])).astype(o_ref.dtype)


def _da_key_positions():
    c = np.zeros((2, DA_BLK, 2 * DA_HEAD_DIM), np.float32)
    c[0, :, 0:DA_SPLIT] = (np.arange(DA_BLK) - DA_BLK // 2)[:, None]
    c[1] = -c[0]
    return jnp.asarray(c, dtype=BF16)


def _da(proj, qkv, batch, seqlen, scal, ng, out_scale):
    t = batch * seqlen
    blk = DA_BLK
    nb = seqlen // blk
    hw = 2 * DA_HEAD_DIM
    w = DA_HPS * hw
    kpos = _da_key_positions()

    def qblk(col):
        return pl.BlockSpec((blk, w), lambda b, h, i, j: (b * nb + i, col // w + h))

    def kblk(col):
        def imap(b, h, i, j):
            return (b * nb + jnp.where(j == 0, i, jnp.where(j - 1 < i, j - 1, j)), col // w + h)
        return pl.BlockSpec((blk, w), imap)

    return pl.pallas_call(
        functools.partial(_da_kernel, out_scale),
        grid=(batch, DA_HEADS // DA_HPS, nb, nb),
        in_specs=[pl.BlockSpec(memory_space=pltpu.SMEM),
                  qblk(COL["q_c"]), kblk(COL["k_c"]), kblk(COL["v_c"]), qblk(COL["g_c"]),
                  pl.BlockSpec((1, hw), lambda b, h, i, j: (0, 0)),
                  pl.BlockSpec(kpos.shape, lambda b, h, i, j: (0, 0, 0))],
        out_specs=pl.BlockSpec((blk, w), lambda b, h, i, j: (b * nb + i, h)),
        out_shape=jax.ShapeDtypeStruct((t, DA_INNER), BF16),
        scratch_shapes=[pltpu.VMEM((DA_HPS, 2, 2 * hw, blk), BF16),
                        pltpu.VMEM((DA_HPS, 2, 1, blk), F32),
                        pltpu.VMEM((DA_HPS, 2, hw + DA_ONES, blk), F32)],
        compiler_params=pltpu.CompilerParams(
            dimension_semantics=("parallel", "parallel", "parallel", "arbitrary"), vmem_limit_bytes=VMEM_LIMIT),
        name="da",
    )(scal, qkv, qkv, qkv, proj, ng, kpos)


def _merge_kernel(final, ya_ref, yb_ref, yc_ref, ga_ref, gb_ref, gc_ref, x_ref, wa_ref, wb_ref, wc_ref, wo_ref,
                  fg_ref, o_ref):
    merged = (_sigmoid(ga_ref[...]) * _dot(ya_ref[...], wa_ref[...])
              + _sigmoid(gb_ref[...]) * _dot(yb_ref[...], wb_ref[...])
              + _sigmoid(gc_ref[...]) * _dot(yc_ref[...], wc_ref[...]))
    out = x_ref[...] + _dot(merged.astype(BF16), wo_ref[...])
    if final:
        out = _rms(out, fg_ref[...])
    o_ref[...] = out


def _merge(ya, yb, yc, proj, x, wa, wb, wc, wo, fg, final, tm=256):
    t = x.shape[0]

    def rows(width, col=0):
        return pl.BlockSpec((tm, width), lambda i: (i, col // width))

    def full(a):
        return pl.BlockSpec(a.shape, lambda i: (0,) * a.ndim)

    return pl.pallas_call(
        functools.partial(_merge_kernel, final),
        grid=(t // tm,),
        in_specs=[rows(SSM_INNER), rows(NA_INNER), rows(DA_INNER),
                  rows(D_MODEL, COL["ga"]), rows(D_MODEL, COL["gb"]), rows(D_MODEL, COL["gc"]),
                  rows(D_MODEL), full(wa), full(wb), full(wc), full(wo), full(fg)],
        out_specs=rows(D_MODEL),
        out_shape=jax.ShapeDtypeStruct((t, D_MODEL), F32),
        compiler_params=pltpu.CompilerParams(dimension_semantics=("parallel",), vmem_limit_bytes=VMEM_LIMIT),
        name="merge",
    )(ya, yb, yc, proj, proj, proj, x, wa, wb, wc, wo, fg)


def _pad_lanes(v, width=LANES):
    v = v.reshape(1, -1).astype(F32)
    return jnp.pad(v, ((0, 0), (0, width - v.shape[1])))


def _layer_params(l, norm_g, w_in, conv_w, conv_b, a_log, dt_bias, d_skip, ssm_norm_g, na_rpb, da_lambda,
                  da_norm_g, w_proj_a, w_proj_b, w_proj_c, w_out):
    def proj_weight(dt):
        w = w_in[l].astype(F32)[:, _PERM[dt]] * _COLSCALE[dt]
        return jnp.pad(w, ((0, 0), (0, _WIDTH[dt] - w.shape[1]))).astype(BF16)

    lam_init = 0.8 - 0.6 * math.exp(-0.3 * l)
    lv = da_lambda[l].astype(F32)
    lam = jnp.exp(jnp.sum(lv[0] * lv[1])) - jnp.exp(jnp.sum(lv[2] * lv[3])) + lam_init
    slopes = (2.0 ** (-8.0 * np.arange(1, DA_HEADS + 1, dtype=np.float64) / DA_HEADS) * LOG2E).astype(np.float32)
    parts, rest = [], jnp.asarray(slopes)
    for _ in range(DA_SPLIT):
        parts.append(rest.astype(BF16).astype(F32))
        rest = rest - parts[-1]
    slopes = jnp.concatenate([jnp.asarray(slopes)] + parts)
    return dict(
        ng=norm_g[l].reshape(1, D_MODEL).astype(F32), w32=proj_weight(F32), w16=proj_weight(BF16),
        conv_w=conv_w[l].astype(F32), conv_b=conv_b[l].reshape(1, CONV_DIM).astype(F32),
        dtb=_pad_lanes(dt_bias[l]), alog=_pad_lanes(a_log[l]),
        dsk=jnp.repeat(d_skip[l].astype(F32), SSM_HEAD_DIM).reshape(1, SSM_INNER),
        sng=ssm_norm_g[l].reshape(1, SSM_INNER).astype(F32),
        bias=_na_bias_table(na_rpb[l]),
        scal=jnp.concatenate([lam.reshape(1), slopes]).astype(F32),
        dng=da_norm_g[l].reshape(1, 2 * DA_HEAD_DIM).astype(F32), out_scale=1.0 - lam_init,
        wa=w_proj_a[l].astype(BF16), wb=w_proj_b[l].astype(BF16), wc=w_proj_c[l].astype(BF16),
        wo=w_out[l].astype(BF16),
    )


def _encoder(x, params, fg):
    batch, seqlen, _ = x.shape
    xt = x.reshape(batch * seqlen, D_MODEL)
    for l, p in enumerate(params):
        proj = _inproj(xt, p["ng"], p["w32"], F32)
        qkv = _inproj(xt, p["ng"], p["w16"], BF16)
        yf, xbc = _ssd(proj, batch, seqlen, False, p["conv_w"], p["conv_b"], p["dtb"], p["alog"], p["dsk"])
        ya = _ssd(proj, batch, seqlen, True, p["conv_w"], p["conv_b"], p["dtb"], p["alog"], p["dsk"],
                  yf=yf, xbc=xbc, ng=p["sng"])
        yb = _na(proj, qkv, batch, seqlen, p["bias"])
        yc = _da(proj, qkv, batch, seqlen, p["scal"], p["dng"], p["out_scale"])
        xt = _merge(ya, yb, yc, proj, xt, p["wa"], p["wb"], p["wc"], p["wo"], fg, l == len(params) - 1)
    return xt.reshape(batch, seqlen, D_MODEL)


def kernel(x_prompt, x_sample, norm_g, w_in, conv_w, conv_b, a_log, dt_bias, d_skip, ssm_norm_g, na_rpb,
           da_lambda, da_norm_g, w_proj_a, w_proj_b, w_proj_c, w_out, final_norm_g):
    params = [_layer_params(l, norm_g, w_in, conv_w, conv_b, a_log, dt_bias, d_skip, ssm_norm_g, na_rpb,
                            da_lambda, da_norm_g, w_proj_a, w_proj_b, w_proj_c, w_out) for l in range(DEPTH)]
    fg = final_norm_g.reshape(1, D_MODEL).astype(F32)
    return (_encoder(x_prompt, params, fg), _encoder(x_sample, params, fg))
```
